```python
import math
import functools
import jax
import jax.numpy as jnp
from jax import lax
import numpy as np

D_MODEL = 2048
BATCH = 4
SEQ = 2048
DEPTH = 4
DEC_BATCH = 8
DEC_SEQ = 8
PAST_LEN = 16384
PAGE_SIZE = 128

BRANCH_W = D_MODEL // 2
A_HD = 64
A_HEADS = BRANCH_W // (2 * A_HD)
A_QK_W = A_HEADS * 2 * A_HD
A_DV = 2 * A_HD
A_V_W = A_HEADS * A_DV
Q_BLOCK = 128
LRU_W = BRANCH_W
LRU_BLOCKS = 16
LRU_BW = LRU_W // LRU_BLOCKS
CONV_W = 4
LRU_C = 8.0
C_W = BRANCH_W
C_GROUPS = 4
C_GW = C_W // C_GROUPS
CHUNK = 128
N_BRANCH = 3
FF_W = 4 * D_MODEL
W_IN = 2 * A_QK_W + A_V_W + LRU_W + 2 * C_W + N_BRANCH * D_MODEL
EPS = 1e-6

kernel_name = 'hybrid_diffattn_rglru_chunkmlp_step'


def rms_norm(x, g):
    xf = x.astype(jnp.float32)
    y = xf * lax.rsqrt(jnp.mean(xf * xf, axis=-1, keepdims=True) + EPS)
    return (y * g.astype(jnp.float32)).astype(x.dtype)


def layer_norm(x, g, b):
    xf = x.astype(jnp.float32)
    xc = xf - jnp.mean(xf, axis=-1, keepdims=True)
    y = xc * lax.rsqrt(jnp.mean(xc * xc, axis=-1, keepdims=True) + EPS)
    return (y * g.astype(jnp.float32) + b.astype(jnp.float32)).astype(x.dtype)


def alibi_slopes():
    return jnp.exp2(-8.0 * jnp.arange(1, A_HEADS + 1, dtype=jnp.float32) / A_HEADS)


def split_proj(xn, w_in):
    p = xn @ w_in
    b, s = p.shape[:2]
    i0 = A_QK_W
    i1 = 2 * A_QK_W
    i2 = i1 + A_V_W
    i3 = i2 + LRU_W
    i4 = i3 + 2 * C_W
    q = p[..., :i0].reshape(b, s, A_HEADS, 2, A_HD)
    k = p[..., i0:i1].reshape(b, s, A_HEADS, 2 * A_HD)
    v = p[..., i1:i2].reshape(b, s, A_HEADS, A_DV)
    xb = p[..., i2:i3]
    z = p[..., i3:i4]
    g = p[..., i4:].reshape(b, s, N_BRANCH, D_MODEL)
    return q, k, v, xb, z, g


def diff_attn_block(q, k, v, q_pos, k_pos, lam, slopes):
    s = jnp.einsum('bqhmd,bkhmd->bhmqk', q, k).astype(jnp.float32) * (A_HD ** -0.5)
    dist = (q_pos[:, None] - k_pos[None, :]).astype(jnp.float32)
    s = s - slopes[None, :, None, None, None] * dist
    s = jnp.where(dist >= 0, s, -jnp.inf)
    p = jax.nn.softmax(s, axis=-1)
    a = p[:, :, 0] - lam.astype(jnp.float32) * p[:, :, 1]
    return jnp.einsum('bhqk,bkhe->bqhe', a.astype(v.dtype), v)


def prompt_attention(q, k, v, lam, slopes):
    b, s = q.shape[:2]
    nb = s // Q_BLOCK
    kh = k.reshape(b, s, A_HEADS, 2, A_HD)
    k_pos = jnp.arange(s)
    qb = jnp.swapaxes(q.reshape(b, nb, Q_BLOCK, A_HEADS, 2, A_HD), 0, 1)
    qpos = k_pos.reshape(nb, Q_BLOCK)

    def one_block(args):
        q_blk, qp = args
        return diff_attn_block(q_blk, kh, v, qp, k_pos, lam, slopes)

    o = lax.map(one_block, (qb, qpos))
    return jnp.swapaxes(o, 0, 1).reshape(b, s, A_HEADS, A_DV)


def sample_attention(q, k, v, lam, past_k, past_v, slopes):
    b, s = q.shape[:2]
    past_len = past_k.shape[1]
    k_all = jnp.concatenate([past_k.astype(k.dtype), k], axis=1).reshape(b, past_len + s, A_HEADS, 2, A_HD)
    v_all = jnp.concatenate([past_v.astype(v.dtype), v], axis=1)
    k_pos = jnp.arange(past_len + s)
    q_pos = past_len + jnp.arange(s)
    return diff_attn_block(q, k_all, v_all, q_pos, k_pos, lam, slopes)


def rg_lru_branch(xb, conv_buf, h0, conv_w, conv_b, w_rg, b_rg, w_ig, b_ig, lru_lambda):
    b, s = xb.shape[:2]
    xpad = jnp.concatenate([conv_buf.astype(xb.dtype), xb], axis=1)
    xc = conv_b + conv_w[0] * xpad[:, 0:s]
    for j in range(1, CONV_W):
        xc = xc + conv_w[j] * xpad[:, j:j + s]
    new_buf = xpad[:, -(CONV_W - 1):]
    xg = xc.reshape(b, s, LRU_BLOCKS, LRU_BW)
    r = jax.nn.sigmoid(jnp.einsum('bsgi,gij->bsgj', xg, w_rg).reshape(b, s, LRU_W) + b_rg)
    i = jax.nn.sigmoid(jnp.einsum('bsgi,gij->bsgj', xg, w_ig).reshape(b, s, LRU_W) + b_ig)
    log_a = -LRU_C * r.astype(jnp.float32) * jax.nn.softplus(-lru_lambda.astype(jnp.float32))
    a = jnp.exp(log_a)
    u = jnp.sqrt(-jnp.expm1(2.0 * log_a)) * (i * xc).astype(jnp.float32)
    u = u.at[:, 0].add(a[:, 0] * h0.astype(jnp.float32))

    def combine(left, right):
        return (left[0] * right[0], right[0] * left[1] + right[1])

    _, h = lax.associative_scan(combine, (a, u), axis=1)
    return h.astype(xb.dtype), new_buf, h[:, -1].astype(xb.dtype)


def chunk_mlp_branch(z, ln_g, ln_b, w_s, b_s):
    b, s = z.shape[:2]
    zg = jax.nn.gelu(z)
    u = zg[..., :C_W]
    v = layer_norm(zg[..., C_W:], ln_g, ln_b)
    t = min(s, CHUNK)
    n = s // t
    v5 = v.reshape(b, n, t, C_GROUPS, C_GW)
    mask = jnp.tril(jnp.ones((t, t), dtype=bool))
    wm = jnp.where(mask[None], w_s[:, :t, :t], 0.0).astype(v.dtype)
    f = jnp.einsum('gts,bnsgd->bntgd', wm, v5) + b_s[:, :t].T[None, None, :, :, None]
    return u * f.reshape(b, s, C_W), v


def merge_branches(g, o_a, o_b, o_c, w_branch, w_out):
    gate = jax.nn.sigmoid(g)
    m = (gate[..., 0, :] * (o_a @ w_branch[0]) + gate[..., 1, :] * (o_b @ w_branch[1])
         + gate[..., 2, :] * (o_c @ w_branch[2]))
    return m @ w_out


def sq_relu_ffn(x, w_ff1, w_ff2):
    return jnp.square(jax.nn.relu(x @ w_ff1)) @ w_ff2


def trunk_layer(h, attend, conv_buf, h0, lam, lam_init, norm1_g, w_in, subln_g, conv_w, conv_b,
                w_rg, b_rg, w_ig, b_ig, lru_lambda, cmlp_ln_g, cmlp_ln_b, w_s, b_s, w_branch, w_out,
                norm2_g, w_ff1, w_ff2):
    b, s = h.shape[:2]
    xn = rms_norm(h, norm1_g)
    q, k, v, xb, z, g = split_proj(xn, w_in)
    o = attend(q, k, v, lam)
    o_a = (rms_norm(o, subln_g) * (1.0 - lam_init)).reshape(b, s, BRANCH_W)
    o_b, new_buf, h_last = rg_lru_branch(xb, conv_buf, h0, conv_w, conv_b, w_rg, b_rg, w_ig, b_ig, lru_lambda)
    o_c, v_c = chunk_mlp_branch(z, cmlp_ln_g, cmlp_ln_b, w_s, b_s)
    h = h + merge_branches(g, o_a, o_b, o_c, w_branch, w_out)
    h = h + sq_relu_ffn(rms_norm(h, norm2_g), w_ff1, w_ff2)
    return h, k, v, new_buf, h_last, v_c


def setup_inputs(seed: int = 0) -> dict:
    key = jax.random.key(seed)
    ks = jax.random.split(key, 32)
    f32 = jnp.float32
    n_pages = PAST_LEN // PAGE_SIZE
    n_used = DEC_BATCH * n_pages
    n_pool = n_used + n_used // 4
    nrm = lambda k, shape, scale: jax.random.normal(k, shape, f32) * scale
    page_table = jax.random.permutation(ks[7], n_pool)[:n_used].reshape(DEC_BATCH, n_pages).astype(jnp.int32)
    a8 = jax.random.uniform(ks[16], (DEPTH, LRU_W), f32, 0.9, 0.999)
    a0 = a8 ** (1.0 / LRU_C)
    lru_lambda = jnp.log(a0) - jnp.log1p(-a0)
    return {
        'x_prompt': nrm(ks[0], (BATCH, SEQ, D_MODEL), 1.0),
        'x_sample': nrm(ks[1], (DEC_BATCH, DEC_SEQ, D_MODEL), 1.0),
        'cache_k': nrm(ks[2], (DEPTH, n_pool, PAGE_SIZE, A_HEADS, 2 * A_HD), 1.0),
        'cache_v': nrm(ks[3], (DEPTH, n_pool, PAGE_SIZE, A_HEADS, A_DV), 1.0),
        'state_h': nrm(ks[4], (DEPTH, DEC_BATCH, LRU_W), 0.5),
        'state_conv': nrm(ks[5], (DEPTH, DEC_BATCH, CONV_W - 1, LRU_W), 1.0),
        'page_table': page_table,
        'norm1_g': 1.0 + nrm(ks[6], (DEPTH, D_MODEL), 0.02),
        'w_in': nrm(ks[8], (DEPTH, D_MODEL, W_IN), D_MODEL ** -0.5),
        'lam_qk': nrm(ks[9], (DEPTH, 4, A_HD), 0.1),
        'subln_g': 1.0 + nrm(ks[10], (DEPTH, A_DV), 0.02),
        'conv_w': nrm(ks[11], (DEPTH, CONV_W, LRU_W), CONV_W ** -0.5),
        'conv_b': nrm(ks[12], (DEPTH, LRU_W), 0.02),
        'w_rg': nrm(ks[13], (DEPTH, LRU_BLOCKS, LRU_BW, LRU_BW), LRU_BW ** -0.5),
        'b_rg': nrm(ks[14], (DEPTH, LRU_W), 0.02),
        'w_ig': nrm(ks[15], (DEPTH, LRU_BLOCKS, LRU_BW, LRU_BW), LRU_BW ** -0.5),
        'b_ig': nrm(ks[17], (DEPTH, LRU_W), 0.02),
        'lru_lambda': lru_lambda,
        'cmlp_ln_g': 1.0 + nrm(ks[18], (DEPTH, C_W), 0.02),
        'cmlp_ln_b': nrm(ks[19], (DEPTH, C_W), 0.02),
        'w_s': nrm(ks[20], (DEPTH, C_GROUPS, CHUNK, CHUNK), CHUNK ** -0.5),
        'b_s': nrm(ks[21], (DEPTH, C_GROUPS, CHUNK), 0.02),
        'w_branch': nrm(ks[22], (DEPTH, N_BRANCH, BRANCH_W, D_MODEL), BRANCH_W ** -0.5),
        'w_out': nrm(ks[23], (DEPTH, D_MODEL, D_MODEL), D_MODEL ** -0.5),
        'norm2_g': 1.0 + nrm(ks[24], (DEPTH, D_MODEL), 0.02),
        'w_ff1': nrm(ks[25], (DEPTH, D_MODEL, FF_W), D_MODEL ** -0.5),
        'w_ff2': nrm(ks[26], (DEPTH, FF_W, D_MODEL), FF_W ** -0.5),
        'final_g': 1.0 + nrm(ks[27], (D_MODEL,), 0.02),
    }


def reference(x_prompt, x_sample, cache_k, cache_v, state_h, state_conv, page_table,
              norm1_g, w_in, lam_qk, subln_g, conv_w, conv_b, w_rg, b_rg, w_ig, b_ig, lru_lambda,
              cmlp_ln_g, cmlp_ln_b, w_s, b_s, w_branch, w_out, norm2_g, w_ff1, w_ff2, final_g):
    slopes = alibi_slopes()
    bp = x_prompt.shape[0]
    bs = x_sample.shape[0]
    past_len = page_table.shape[1] * PAGE_SIZE
    hp = x_prompt
    hs = x_sample
    kp_l, vp_l, hp_l, cp_l = [], [], [], []
    ks_l, vs_l, hs_l, cs_l, vc_l = [], [], [], [], []
    attend_p = functools.partial(prompt_attention, slopes=slopes)
    for l in range(DEPTH):
        lam_init = 0.8 - 0.6 * math.exp(-0.3 * l)
        lq = lam_qk[l].astype(jnp.float32)
        lam = jnp.exp(jnp.sum(lq[0] * lq[1])) - jnp.exp(jnp.sum(lq[2] * lq[3])) + lam_init
        lw = (norm1_g[l], w_in[l], subln_g[l], conv_w[l], conv_b[l], w_rg[l], b_rg[l], w_ig[l], b_ig[l],
              lru_lambda[l], cmlp_ln_g[l], cmlp_ln_b[l], w_s[l], b_s[l], w_branch[l], w_out[l],
              norm2_g[l], w_ff1[l], w_ff2[l])
        zero_buf = jnp.zeros((bp, CONV_W - 1, LRU_W), hp.dtype)
        zero_h = jnp.zeros((bp, LRU_W), hp.dtype)
        hp, kp, vp, cp, hlp, _ = trunk_layer(hp, attend_p, zero_buf, zero_h, lam, lam_init, *lw)
        past_k = cache_k[l][page_table].reshape(bs, past_len, A_HEADS, 2 * A_HD)
        past_v = cache_v[l][page_table].reshape(bs, past_len, A_HEADS, A_DV)
        attend_s = functools.partial(sample_attention, past_k=past_k, past_v=past_v, slopes=slopes)
        hs, k_s, v_s, c_s, h_s, vc_s = trunk_layer(hs, attend_s, state_conv[l], state_h[l], lam, lam_init, *lw)
        kp_l.append(kp)
        vp_l.append(vp)
        hp_l.append(hlp)
        cp_l.append(cp)
        ks_l.append(k_s)
        vs_l.append(v_s)
        hs_l.append(h_s)
        cs_l.append(c_s)
        vc_l.append(vc_s)
    y_prompt = rms_norm(hp, final_g)
    y_sample = rms_norm(hs, final_g)
    k_prompt = jnp.stack(kp_l)
    v_prompt = jnp.stack(vp_l)
    h_prompt = jnp.stack(hp_l)
    conv_prompt = jnp.stack(cp_l)
    k_sample = jnp.stack(ks_l)
    v_sample = jnp.stack(vs_l)
    h_sample = jnp.stack(hs_l)
    conv_sample = jnp.stack(cs_l)
    cmlp_v_sample = jnp.stack(vc_l)
    return (y_prompt, y_sample, k_prompt, v_prompt, h_prompt, conv_prompt, k_sample, v_sample, h_sample, conv_sample, cmlp_v_sample)
```

```python
import functools
import math

import jax
import jax.numpy as jnp
from jax import lax
from jax.experimental import pallas as pl
from jax.experimental.pallas import tpu as pltpu

F32 = jnp.float32
BF16 = jnp.bfloat16

D_MODEL = 2048
DEPTH = 4
PAGE_SIZE = 128
BRANCH_W = D_MODEL // 2
A_HD = 64
A_HEADS = BRANCH_W // (2 * A_HD)
A_DV = 2 * A_HD
LRU_W = BRANCH_W
LRU_BLOCKS = 16
LRU_BW = LRU_W // LRU_BLOCKS
CONV_W = 4
LRU_C = 8.0
C_W = BRANCH_W
C_GROUPS = 4
C_GW = C_W // C_GROUPS
CHUNK = 128
N_BRANCH = 3
FF_W = 4 * D_MODEL
EPS = 1e-6

COL_Q = 0
COL_K = BRANCH_W
COL_V = 2 * BRANCH_W
COL_XB = 3 * BRANCH_W
COL_Z = 4 * BRANCH_W
COL_G = 6 * BRANCH_W
W_IN = COL_G + N_BRANCH * D_MODEL

LANES = 128
SUBLANES = 8
NEG_BIG = -1e30
MIB = 1024 * 1024
VMEM_LIMIT = 52 * MIB


def _params(semantics):
    return pltpu.CompilerParams(dimension_semantics=semantics, vmem_limit_bytes=VMEM_LIMIT)


def _rms(x, g):
    return x * lax.rsqrt(jnp.mean(x * x, axis=-1, keepdims=True) + EPS) * g


def _norm_mm_kernel(x_ref, g_ref, w_ref, o_ref, xn_ref):
    @pl.when(pl.program_id(1) == 0)
    def _():
        xn_ref[...] = _rms(x_ref[...], g_ref[...]).astype(BF16)

    o_ref[...] = jnp.dot(xn_ref[...], w_ref[...], preferred_element_type=F32)


def norm_matmul(x, g3, w3, layer, *, tm, tn):
    m, k = x.shape
    n = w3.shape[-1]
    return pl.pallas_call(
        _norm_mm_kernel,
        grid=(m // tm, n // tn),
        in_specs=[
            pl.BlockSpec((tm, k), lambda i, j: (i, 0)),
            pl.BlockSpec((None, 1, k), lambda i, j: (layer, 0, 0)),
            pl.BlockSpec((None, k, tn), lambda i, j: (layer, 0, j)),
        ],
        out_specs=pl.BlockSpec((tm, tn), lambda i, j: (i, j)),
        out_shape=jax.ShapeDtypeStruct((m, n), F32),
        scratch_shapes=[pltpu.VMEM((tm, k), BF16)],
        compiler_params=_params(("parallel", "arbitrary")),
        name="in_proj",
    )(x, g3, w3)


def _lambda_value(lq, lam_init):
    a = jnp.sum(lq[0:1] * lq[1:2], axis=-1, keepdims=True)
    b = jnp.sum(lq[2:3] * lq[3:4], axis=-1, keepdims=True)
    return jnp.exp(a) - jnp.exp(b) + lam_init


def _online_softmax_update(s, off, vblk, m_ref, l_ref, acc_ref):
    m_old = m_ref[...]
    m_new = jnp.maximum(m_old, jnp.max(s, axis=-1, keepdims=True) + off)
    p = jnp.exp(s - (m_new - off))
    alpha = jnp.exp(m_old - m_new)
    l_ref[...] = alpha * l_ref[...] + jnp.sum(p, axis=-1, keepdims=True)
    acc_ref[...] = alpha * acc_ref[...] + jnp.dot(
        p.astype(BF16), vblk, preferred_element_type=F32)
    m_ref[...] = m_new


_NT_DIMS = (((1,), (1,)), ((), ()))


def _attn_prompt_kernel(slopes_ref, q_ref, k_ref, v_ref, lq_ref, sg_ref, o_ref,
                        kb_ref, vb_ref, sb_ref, sbd_ref, m_ref, l_ref, acc_ref,
                        *, tq, lam_init):
    h = pl.program_id(1)
    qi = pl.program_id(2)
    slope = slopes_ref[h]

    @pl.when(qi == 0)
    def _():
        kb_ref[...] = k_ref[...].astype(BF16)
        vb_ref[...] = v_ref[...].astype(BF16)
        r = lax.broadcasted_iota(jnp.int32, (2 * tq, tq), 0)
        c = lax.broadcasted_iota(jnp.int32, (2 * tq, tq), 1)
        r = jnp.where(r >= tq, r - tq, r)
        base = (c - r).astype(F32) * slope
        sb_ref[...] = base
        sbd_ref[...] = jnp.where(c <= r, base, NEG_BIG)

    q = q_ref[...] * (A_HD ** -0.5)
    lane = lax.broadcasted_iota(jnp.int32, q.shape, 1)
    qs = jnp.concatenate(
        [jnp.where(lane < A_HD, q, 0.0), jnp.where(lane >= A_HD, q, 0.0)],
        axis=0).astype(BF16)
    m_ref[...] = jnp.full(m_ref.shape, NEG_BIG, F32)
    l_ref[...] = jnp.zeros(l_ref.shape, F32)
    acc_ref[...] = jnp.zeros(acc_ref.shape, F32)

    def step(j, sbias, off):
        start = pl.multiple_of(j * tq, tq)
        kblk = kb_ref[pl.ds(start, tq), :]
        vblk = vb_ref[pl.ds(start, tq), :]
        s = lax.dot_general(qs, kblk, _NT_DIMS, preferred_element_type=F32) + sbias
        _online_softmax_update(s, off, vblk, m_ref, l_ref, acc_ref)

    def body(j, carry):
        off = slope * ((j - qi) * tq).astype(F32)
        step(j, sb_ref[...], off)
        return carry

    lax.fori_loop(0, qi, body, 0)
    step(qi, sbd_ref[...], 0.0)

    acc = acc_ref[...]
    l = l_ref[...]
    lam = _lambda_value(lq_ref[...], lam_init)
    o = acc[:tq] / l[:tq] - lam * (acc[tq:] / l[tq:])
    o_ref[...] = (_rms(o, sg_ref[...]) * (1.0 - lam_init)).astype(BF16)


def attention_prompt(p3, slopes, lam_qk, subln_g3, layer, lam_init, *, tq):
    b, s, _ = p3.shape
    hw = 2 * A_HD
    kern = functools.partial(_attn_prompt_kernel, tq=tq, lam_init=lam_init)
    return pl.pallas_call(
        kern,
        grid=(b, A_HEADS, s // tq),
        in_specs=[
            pl.BlockSpec(memory_space=pltpu.SMEM),
            pl.BlockSpec((None, tq, hw), lambda bi, h, qi: (bi, qi, COL_Q // hw + h)),
            pl.BlockSpec((None, s, hw), lambda bi, h, qi: (bi, 0, COL_K // hw + h)),
            pl.BlockSpec((None, s, hw), lambda bi, h, qi: (bi, 0, COL_V // hw + h)),
            pl.BlockSpec((None, 4, A_HD), lambda bi, h, qi: (layer, 0, 0)),
            pl.BlockSpec((None, 1, A_DV), lambda bi, h, qi: (layer, 0, 0)),
        ],
        out_specs=pl.BlockSpec((None, tq, hw), lambda bi, h, qi: (bi, qi, h)),
        out_shape=jax.ShapeDtypeStruct((b, s, BRANCH_W), BF16),
        scratch_shapes=[
            pltpu.VMEM((s, hw), BF16),
            pltpu.VMEM((s, hw), BF16),
            pltpu.VMEM((2 * tq, tq), F32),
            pltpu.VMEM((2 * tq, tq), F32),
            pltpu.VMEM((2 * tq, 1), F32),
            pltpu.VMEM((2 * tq, 1), F32),
            pltpu.VMEM((2 * tq, hw), F32),
        ],
        compiler_params=_params(("parallel", "arbitrary", "arbitrary")),
        name="attn_prompt",
    )(slopes, p3, p3, p3, lam_qk, subln_g3)


def _attn_sample_kernel(pt_ref, q_ref, kn_ref, vn_ref, slope_ref, lq_ref, sg_ref, *rest,
                        n_pages, n_steps, past_len, lam_init):
    del pt_ref
    kp_refs = rest[:n_pages]
    vp_refs = rest[n_pages:2 * n_pages]
    o_ref, qall_ref, sb_ref, m_ref, l_ref, acc_ref = rest[2 * n_pages:]
    s_idx = pl.program_id(1)
    n_rows = 2 * A_HEADS * SUBLANES
    half = A_HEADS * SUBLANES
    span = n_pages * PAGE_SIZE
    slope = slope_ref[...]

    def head_cols(ref, h):
        return ref[:, h * A_DV:(h + 1) * A_DV]

    @pl.when(s_idx == 0)
    def _():
        lane = lax.broadcasted_iota(jnp.int32, (SUBLANES, A_DV), 1)
        pieces = []
        for mp in range(2):
            keep = (lane >= A_HD) if mp else (lane < A_HD)
            for h in range(A_HEADS):
                pieces.append(jnp.where(keep, head_cols(q_ref, h) * (A_HD ** -0.5), 0.0))
        qall_ref[...] = jnp.concatenate(pieces, axis=0).astype(BF16)
        r = lax.broadcasted_iota(jnp.int32, sb_ref.shape, 0)
        c = lax.broadcasted_iota(jnp.int32, sb_ref.shape, 1)
        bias = slope * ((c >> 3) - (r & 7)).astype(F32)
        sb_ref[...] = jnp.where((c & 7) == ((r >> 3) & 7), bias, NEG_BIG)
        m_ref[...] = jnp.full(m_ref.shape, NEG_BIG, F32)
        l_ref[...] = jnp.zeros(l_ref.shape, F32)
        acc_ref[...] = jnp.zeros(acc_ref.shape, F32)

    qall = qall_ref[...]
    kcat = jnp.concatenate([r[...].astype(BF16) for r in kp_refs], axis=0)
    vcat = jnp.concatenate([r[...].astype(BF16) for r in vp_refs], axis=0)
    s = lax.dot_general(qall, kcat, _NT_DIMS, preferred_element_type=F32) + sb_ref[...]
    off = slope * (s_idx * span - past_len).astype(F32)
    _online_softmax_update(s, off, vcat, m_ref, l_ref, acc_ref)

    @pl.when(s_idx == n_steps - 1)
    def _():
        pad = jnp.zeros((PAGE_SIZE - half, A_DV), F32)
        kn = jnp.concatenate([head_cols(kn_ref, h) for h in range(A_HEADS)] + [pad], axis=0)
        vn = jnp.concatenate([head_cols(vn_ref, h) for h in range(A_HEADS)] + [pad], axis=0)
        sn = lax.dot_general(qall, kn.astype(BF16), _NT_DIMS, preferred_element_type=F32)
        r = lax.broadcasted_iota(jnp.int32, sn.shape, 0)
        c = lax.broadcasted_iota(jnp.int32, sn.shape, 1)
        qpos = r & 7
        kpos = c & 7
        valid = (c < half) & ((c >> 3) == ((r >> 3) & 7)) & (kpos <= qpos)
        sn = jnp.where(valid, sn + slope * (kpos - qpos).astype(F32), NEG_BIG)
        _online_softmax_update(sn, 0.0, vn.astype(BF16), m_ref, l_ref, acc_ref)

        o_map = acc_ref[...] / l_ref[...]
        lam = _lambda_value(lq_ref[...], lam_init)
        outs = []
        for h in range(A_HEADS):
            o = (o_map[h * SUBLANES:(h + 1) * SUBLANES]
                 - lam * o_map[half + h * SUBLANES:half + (h + 1) * SUBLANES])
            outs.append(_rms(o, sg_ref[...]) * (1.0 - lam_init))
        o_ref[...] = jnp.concatenate(outs, axis=1).astype(BF16)


def attention_sample(p3, cache_k4, cache_v4, page_table, slope_rows, lam_qk, subln_g3,
                     layer, lam_init, *, n_pages):
    b, t, _ = p3.shape
    assert t == SUBLANES
    pages_total = page_table.shape[1]
    n_steps = pages_total // n_pages
    n_rows = 2 * A_HEADS * SUBLANES
    page_rows = PAGE_SIZE * A_HEADS
    kern = functools.partial(
        _attn_sample_kernel, n_pages=n_pages, n_steps=n_steps,
        past_len=pages_total * PAGE_SIZE, lam_init=lam_init)

    def page_spec(i):
        return pl.BlockSpec(
            (None, None, page_rows, A_DV),
            lambda bi, si, pt: (layer, pt[bi, si * n_pages + i], 0, 0))

    col = lambda c: pl.BlockSpec((None, t, BRANCH_W), lambda bi, si, pt: (bi, 0, c // BRANCH_W))
    grid_spec = pltpu.PrefetchScalarGridSpec(
        num_scalar_prefetch=1,
        grid=(b, n_steps),
        in_specs=[
            col(COL_Q), col(COL_K), col(COL_V),
            pl.BlockSpec((n_rows, 1), lambda bi, si, pt: (0, 0)),
            pl.BlockSpec((None, 4, A_HD), lambda bi, si, pt: (layer, 0, 0)),
            pl.BlockSpec((None, 1, A_DV), lambda bi, si, pt: (layer, 0, 0)),
        ] + [page_spec(i) for i in range(n_pages)] * 2,
        out_specs=pl.BlockSpec((None, t, BRANCH_W), lambda bi, si, pt: (bi, 0, 0)),
        scratch_shapes=[
            pltpu.VMEM((n_rows, A_DV), BF16),
            pltpu.VMEM((n_rows, n_pages * page_rows), F32),
            pltpu.VMEM((n_rows, 1), F32),
            pltpu.VMEM((n_rows, 1), F32),
            pltpu.VMEM((n_rows, A_DV), F32),
        ],
    )
    return pl.pallas_call(
        kern,
        grid_spec=grid_spec,
        out_shape=jax.ShapeDtypeStruct((b, t, BRANCH_W), BF16),
        compiler_params=_params(("parallel", "arbitrary")),
        name="attn_sample",
    )(page_table, p3, p3, p3, slope_rows, lam_qk, subln_g3,
      *([cache_k4] * n_pages), *([cache_v4] * n_pages))


def _shift_rows(x, prev, j, row8):
    xr = pltpu.roll(x, j, 0)
    head = jnp.where(row8 < j, pltpu.roll(prev, j, 0), xr[:SUBLANES])
    if x.shape[0] == SUBLANES:
        return head
    return jnp.concatenate([head, xr[SUBLANES:]], axis=0)


def _lru_kernel(x_ref, c0_ref, h0_ref, cw_ref, cb_ref, wr_ref, wi_ref, br_ref, bi_ref, lam_ref,
                hs_ref, hl_ref, co_ref, hprev_ref, xprev_ref):
    t_idx = pl.program_id(2)

    @pl.when(t_idx == 0)
    def _():
        hprev_ref[...] = h0_ref[...]
        xprev_ref[...] = c0_ref[...]

    x = x_ref[...]
    n_t, tc = x.shape
    prev = xprev_ref[...]
    row8 = lax.broadcasted_iota(jnp.int32, (SUBLANES, tc), 0)
    cw = cw_ref[...]
    xc = cb_ref[...] + cw[0:1] * _shift_rows(x, prev, 3, row8)
    xc = xc + cw[1:2] * _shift_rows(x, prev, 2, row8)
    xc = xc + cw[2:3] * _shift_rows(x, prev, 1, row8)
    xc = xc + cw[3:4] * x
    tail = x[n_t - SUBLANES:]
    xprev_ref[...] = tail
    co_ref[...] = tail

    xcb = xc.astype(BF16)
    r = jax.nn.sigmoid(jnp.dot(xcb, wr_ref[...], preferred_element_type=F32) + br_ref[...])
    i = jax.nn.sigmoid(jnp.dot(xcb, wi_ref[...], preferred_element_type=F32) + bi_ref[...])
    nl = -lam_ref[...]
    softplus = jnp.maximum(nl, 0.0) + jnp.log1p(jnp.exp(-jnp.abs(nl)))
    log_a = (-LRU_C) * r * softplus
    a = jnp.exp(log_a)
    th = jnp.tanh(log_a)
    u = jnp.sqrt(-2.0 * th / (1.0 - th)) * (i * xc)

    row = lax.broadcasted_iota(jnp.int32, (n_t, tc), 0)
    d = 1
    while d < n_t:
        if d < SUBLANES:
            a_sh = jnp.where(row >= d, pltpu.roll(a, d, 0), 1.0)
            u_sh = jnp.where(row >= d, pltpu.roll(u, d, 0), 0.0)
        else:
            a_sh = jnp.concatenate([jnp.ones((d, tc), F32), a[:n_t - d]], axis=0)
            u_sh = jnp.concatenate([jnp.zeros((d, tc), F32), u[:n_t - d]], axis=0)
        u = u + a * u_sh
        a = a * a_sh
        d *= 2
    hseq = u + a * hprev_ref[...]
    hs_ref[...] = hseq.astype(BF16)
    last = hseq[n_t - 1:]
    hprev_ref[...] = last
    hl_ref[...] = last


def rg_lru(p3, conv0, h0, conv_w, conv_b3, wr_bd, wi_bd, b_rg3, b_ig3, lam3, layer, *, tt, tc):
    b, s, _ = p3.shape
    c = LRU_W
    vec = lambda: pl.BlockSpec((None, 1, tc), lambda bi, ci, ti: (layer, 0, ci))
    gate_w = lambda: pl.BlockSpec((None, None, tc, tc), lambda bi, ci, ti: (layer, ci, 0, 0))
    state = lambda rows: pl.BlockSpec((None, rows, tc), lambda bi, ci, ti: (bi, 0, ci))
    return pl.pallas_call(
        _lru_kernel,
        grid=(b, c // tc, s // tt),
        in_specs=[
            pl.BlockSpec((None, tt, tc), lambda bi, ci, ti: (bi, ti, COL_XB // tc + ci)),
            state(SUBLANES), state(1),
            pl.BlockSpec((None, CONV_W, tc), lambda bi, ci, ti: (layer, 0, ci)),
            vec(), gate_w(), gate_w(), vec(), vec(), vec(),
        ],
        out_specs=[
            pl.BlockSpec((None, tt, tc), lambda bi, ci, ti: (bi, ti, ci)),
            state(1), state(SUBLANES),
        ],
        out_shape=[
            jax.ShapeDtypeStruct((b, s, c), BF16),
            jax.ShapeDtypeStruct((b, 1, c), F32),
            jax.ShapeDtypeStruct((b, SUBLANES, c), F32),
        ],
        scratch_shapes=[pltpu.VMEM((1, tc), F32), pltpu.VMEM((SUBLANES, tc), F32)],
        compiler_params=_params(("parallel", "parallel", "arbitrary")),
        name="rg_lru",
    )(p3, conv0, h0, conv_w, conv_b3, wr_bd, wi_bd, b_rg3, b_ig3, lam3)


def _gelu_tanh(x):
    c = math.sqrt(2.0 / math.pi)
    return 0.5 * x * (1.0 + jnp.tanh(c * (x + 0.044715 * (x * x * x))))


def _cmlp_kernel(zu_ref, zv_ref, lg_ref, lb_ref, ws_ref, bst_ref, oc_ref, *maybe_vc_ref, ch):
    u = _gelu_tanh(zu_ref[...])
    vg = _gelu_tanh(zv_ref[...])
    xc = vg - jnp.mean(vg, axis=-1, keepdims=True)
    v = xc * lax.rsqrt(jnp.mean(xc * xc, axis=-1, keepdims=True) + EPS) * lg_ref[...] + lb_ref[...]
    if maybe_vc_ref:
        maybe_vc_ref[0][...] = v
    n_ch = u.shape[0] // ch
    r = lax.broadcasted_iota(jnp.int32, (ch, ch), 0)
    c = lax.broadcasted_iota(jnp.int32, (ch, ch), 1)
    for g in range(C_GROUPS):
        wm = jnp.where(c <= r, ws_ref[g][:ch, :ch], 0.0)
        bias = bst_ref[:ch, g:g + 1]
        cs = slice(g * C_GW, (g + 1) * C_GW)
        for n in range(n_ch):
            rs = slice(n * ch, (n + 1) * ch)
            vb = v[rs, cs]
            if ch >= LANES:
                f = jnp.dot(wm.astype(BF16), vb.astype(BF16), preferred_element_type=F32) + bias
            else:
                f = bias + wm[:, 0:1] * vb[0:1, :]
                for s in range(1, ch):
                    f = f + wm[:, s:s + 1] * vb[s:s + 1, :]
            oc_ref[rs, cs] = (u[rs, cs] * f).astype(BF16)


def chunk_mlp(p, ln_g3, ln_b3, w_s, b_st, layer, *, tm, ch, emit_v):
    m = p.shape[0]
    vec = lambda: pl.BlockSpec((None, 1, C_W), lambda i: (layer, 0, 0))
    out_specs = [pl.BlockSpec((tm, C_W), lambda i: (i, 0))]
    out_shape = [jax.ShapeDtypeStruct((m, C_W), BF16)]
    if emit_v:
        out_specs.append(pl.BlockSpec((tm, C_W), lambda i: (i, 0)))
        out_shape.append(jax.ShapeDtypeStruct((m, C_W), F32))
    return pl.pallas_call(
        functools.partial(_cmlp_kernel, ch=ch),
        grid=(m // tm,),
        in_specs=[
            pl.BlockSpec((tm, C_W), lambda i: (i, COL_Z // C_W)),
            pl.BlockSpec((tm, C_W), lambda i: (i, COL_Z // C_W + 1)),
            vec(), vec(),
            pl.BlockSpec((None, C_GROUPS, CHUNK, CHUNK), lambda i: (layer, 0, 0, 0)),
            pl.BlockSpec((None, CHUNK, C_GROUPS), lambda i: (layer, 0, 0)),
        ],
        out_specs=out_specs,
        out_shape=out_shape,
        compiler_params=_params(("parallel",)),
        name="chunk_mlp",
    )(p, p, ln_g3, ln_b3, w_s, b_st)


def _branch_kernel(oa_ref, ob_ref, oc_ref, g0_ref, g1_ref, g2_ref, wb_ref, m_ref):
    acc = None
    for j, (o_ref, g_ref) in enumerate(((oa_ref, g0_ref), (ob_ref, g1_ref), (oc_ref, g2_ref))):
        y = jnp.dot(o_ref[...], wb_ref[j], preferred_element_type=F32)
        t = jax.nn.sigmoid(g_ref[...]) * y
        acc = t if acc is None else acc + t
    m_ref[...] = acc.astype(BF16)


def branch_merge(o_a, o_b, o_c, p, wb, layer, *, tm, tn):
    m = o_a.shape[0]
    o_spec = lambda: pl.BlockSpec((tm, BRANCH_W), lambda i, j: (i, 0))
    g_spec = lambda b: pl.BlockSpec((tm, tn), lambda i, j: (i, (COL_G + b * D_MODEL) // tn + j))
    return pl.pallas_call(
        _branch_kernel,
        grid=(m // tm, D_MODEL // tn),
        in_specs=[o_spec(), o_spec(), o_spec(), g_spec(0), g_spec(1), g_spec(2),
                  pl.BlockSpec((None, N_BRANCH, BRANCH_W, tn), lambda i, j: (layer, 0, 0, j))],
        out_specs=pl.BlockSpec((tm, tn), lambda i, j: (i, j)),
        out_shape=jax.ShapeDtypeStruct((m, D_MODEL), BF16),
        compiler_params=_params(("parallel", "arbitrary")),
        name="branch_merge",
    )(o_a, o_b, o_c, p, p, p, wb)


def _mm_res_kernel(x_ref, w_ref, h_ref, o_ref):
    o_ref[...] = h_ref[...] + jnp.dot(x_ref[...], w_ref[...], preferred_element_type=F32)


def matmul_residual(x, w3, h, layer, *, tm, tn):
    m, k = x.shape
    n = w3.shape[-1]
    return pl.pallas_call(
        _mm_res_kernel,
        grid=(m // tm, n // tn),
        in_specs=[
            pl.BlockSpec((tm, k), lambda i, j: (i, 0)),
            pl.BlockSpec((None, k, tn), lambda i, j: (layer, 0, j)),
            pl.BlockSpec((tm, tn), lambda i, j: (i, j)),
        ],
        out_specs=pl.BlockSpec((tm, tn), lambda i, j: (i, j)),
        out_shape=jax.ShapeDtypeStruct((m, n), F32),
        compiler_params=_params(("parallel", "arbitrary")),
        name="out_proj",
    )(x, w3, h)


def _ffn_kernel(h_ref, g_ref, w1_ref, w2_ref, o_ref, xn_ref):
    @pl.when(pl.program_id(1) == 0)
    def _():
        h = h_ref[...]
        xn_ref[...] = _rms(h, g_ref[...]).astype(BF16)
        o_ref[...] = h

    hid = jnp.maximum(jnp.dot(xn_ref[...], w1_ref[...], preferred_element_type=F32), 0.0)
    o_ref[...] += jnp.dot((hid * hid).astype(BF16), w2_ref[...], preferred_element_type=F32)


def ffn(h, g3, w1, w2, layer, *, tm, tf):
    m, d = h.shape
    return pl.pallas_call(
        _ffn_kernel,
        grid=(m // tm, FF_W // tf),
        in_specs=[
            pl.BlockSpec((tm, d), lambda i, j: (i, 0)),
            pl.BlockSpec((None, 1, d), lambda i, j: (layer, 0, 0)),
            pl.BlockSpec((None, d, tf), lambda i, j: (layer, 0, j)),
            pl.BlockSpec((None, tf, d), lambda i, j: (layer, j, 0)),
        ],
        out_specs=pl.BlockSpec((tm, d), lambda i, j: (i, 0)),
        out_shape=jax.ShapeDtypeStruct((m, d), F32),
        scratch_shapes=[pltpu.VMEM((tm, d), BF16)],
        compiler_params=_params(("parallel", "arbitrary")),
        name="ffn",
    )(h, g3, w1, w2)


def _final_norm_kernel(x_ref, g_ref, o_ref):
    o_ref[...] = _rms(x_ref[...], g_ref[...])


def final_norm(x, g2, *, tm):
    m, d = x.shape
    return pl.pallas_call(
        _final_norm_kernel,
        grid=(m // tm,),
        in_specs=[pl.BlockSpec((tm, d), lambda i: (i, 0)), pl.BlockSpec((1, d), lambda i: (0, 0))],
        out_specs=pl.BlockSpec((tm, d), lambda i: (i, 0)),
        out_shape=jax.ShapeDtypeStruct((m, d), F32),
        compiler_params=_params(("parallel",)),
        name="final_norm",
    )(x, g2)


def _block_diag_gates(w):
    per = 256 // LRU_BW
    w5 = w.reshape(DEPTH, LRU_BLOCKS // per, per, LRU_BW, LRU_BW)
    eye = jnp.eye(per, dtype=w.dtype)
    bd = jnp.einsum("lgbij,bc->lgbicj", w5, eye)
    return bd.reshape(DEPTH, LRU_BLOCKS // per, 256, 256).astype(BF16)


def _trunk_layer(h, layer, wts, attend, conv0, h0, *, tiles, ch, emit_v):
    bsz, seq, _ = h.shape
    m = bsz * seq
    h2 = h.reshape(m, D_MODEL)
    p = norm_matmul(h2, wts["norm1_g"], wts["w_in"], layer, tm=tiles["tm"], tn=tiles["tn_in"])
    p3 = p.reshape(bsz, seq, W_IN)
    o_a = attend(p3).reshape(m, BRANCH_W)
    o_b, h_last, conv_tail = rg_lru(
        p3, conv0, h0, wts["conv_w"], wts["conv_b"], wts["w_rg"], wts["w_ig"],
        wts["b_rg"], wts["b_ig"], wts["lru_lambda"], layer, tt=tiles["tt"], tc=256)
    cm = chunk_mlp(p, wts["cmlp_ln_g"], wts["cmlp_ln_b"], wts["w_s"], wts["b_st"], layer,
                   tm=tiles["tm_c"], ch=ch, emit_v=emit_v)
    mrg = branch_merge(o_a, o_b.reshape(m, BRANCH_W), cm[0], p, wts["w_branch"], layer,
                       tm=tiles["tm"], tn=tiles["tn"])
    h2 = matmul_residual(mrg, wts["w_out"], h2, layer, tm=tiles["tm"], tn=tiles["tn"])
    h2 = ffn(h2, wts["norm2_g"], wts["w_ff1"], wts["w_ff2"], layer, tm=tiles["tm_f"], tf=tiles["tf"])
    k = p3[..., COL_K:COL_V].reshape(bsz, seq, A_HEADS, 2 * A_HD)
    v = p3[..., COL_V:COL_XB].reshape(bsz, seq, A_HEADS, A_DV)
    new_buf = conv_tail[:, SUBLANES - (CONV_W - 1):]
    v_c = cm[1].reshape(bsz, seq, C_W) if emit_v else None
    return h2.reshape(bsz, seq, D_MODEL), k, v, new_buf, h_last[:, 0], v_c


def kernel(x_prompt, x_sample, cache_k, cache_v, state_h, state_conv, page_table,
           norm1_g, w_in, lam_qk, subln_g, conv_w, conv_b, w_rg, b_rg, w_ig, b_ig, lru_lambda,
           cmlp_ln_g, cmlp_ln_b, w_s, b_s, w_branch, w_out, norm2_g, w_ff1, w_ff2, final_g):
    bp, sp, _ = x_prompt.shape
    bs, ss, _ = x_sample.shape
    n_pool = cache_k.shape[1]
    row = lambda a: a.reshape(DEPTH, 1, a.shape[-1])
    wts = {
        "norm1_g": row(norm1_g), "w_in": w_in.astype(BF16),
        "conv_w": conv_w, "conv_b": row(conv_b),
        "w_rg": _block_diag_gates(w_rg), "w_ig": _block_diag_gates(w_ig),
        "b_rg": row(b_rg), "b_ig": row(b_ig), "lru_lambda": row(lru_lambda),
        "cmlp_ln_g": row(cmlp_ln_g), "cmlp_ln_b": row(cmlp_ln_b),
        "w_s": w_s, "b_st": jnp.swapaxes(b_s, 1, 2),
        "w_branch": w_branch.astype(BF16), "w_out": w_out.astype(BF16),
        "norm2_g": row(norm2_g), "w_ff1": w_ff1.astype(BF16), "w_ff2": w_ff2.astype(BF16),
    }
    subln_g3 = row(subln_g)
    slopes = jnp.exp2(-8.0 * jnp.arange(1, A_HEADS + 1, dtype=F32) / A_HEADS)
    slope_rows = jnp.tile(jnp.repeat(slopes, SUBLANES), 2).reshape(2 * A_HEADS * SUBLANES, 1)
    cache_k4 = cache_k.reshape(DEPTH, n_pool, PAGE_SIZE * A_HEADS, A_DV)
    cache_v4 = cache_v.reshape(DEPTH, n_pool, PAGE_SIZE * A_HEADS, A_DV)
    conv0_p = jnp.zeros((bp, SUBLANES, LRU_W), F32)
    h0_p = jnp.zeros((bp, 1, LRU_W), F32)
    conv0_s = jnp.pad(state_conv, ((0, 0), (0, 0), (SUBLANES - (CONV_W - 1), 0), (0, 0)))

    tiles_p = dict(tm=1024, tn_in=512, tn=512, tt=256, tm_c=512, tm_f=512, tf=1024)
    ms = bs * ss
    tiles_s = dict(tm=ms, tn_in=512, tn=512, tt=ss, tm_c=ms, tm_f=ms, tf=1024)

    hp, hs = x_prompt, x_sample
    outs = [[] for _ in range(9)]
    for l in range(DEPTH):
        lam_init = 0.8 - 0.6 * math.exp(-0.3 * l)
        attend_p = functools.partial(
            attention_prompt, slopes=slopes, lam_qk=lam_qk, subln_g3=subln_g3,
            layer=l, lam_init=lam_init, tq=256)
        hp, kp, vp, cp, hlp, _ = _trunk_layer(
            hp, l, wts, attend_p, conv0_p, h0_p, tiles=tiles_p, ch=CHUNK, emit_v=False)
        attend_s = functools.partial(
            attention_sample, cache_k4=cache_k4, cache_v4=cache_v4, page_table=page_table,
            slope_rows=slope_rows, lam_qk=lam_qk, subln_g3=subln_g3,
            layer=l, lam_init=lam_init, n_pages=4)
        hs, k_s, v_s, c_s, h_s, vc_s = _trunk_layer(
            hs, l, wts, attend_s, conv0_s[l], state_h[l][:, None, :],
            tiles=tiles_s, ch=min(ss, CHUNK), emit_v=True)
        for lst, val in zip(outs, (kp, vp, hlp, cp, k_s, v_s, h_s, c_s, vc_s)):
            lst.append(val)
    y_prompt = final_norm(hp.reshape(bp * sp, D_MODEL), final_g.reshape(1, D_MODEL), tm=1024)
    y_sample = final_norm(hs.reshape(ms, D_MODEL), final_g.reshape(1, D_MODEL), tm=ms)
    stacked = [jnp.stack(lst) for lst in outs]
    return (y_prompt.reshape(bp, sp, D_MODEL), y_sample.reshape(bs, ss, D_MODEL), *stacked)
```

```python
import functools
import math

import jax
import jax.numpy as jnp
from jax import lax
from jax.experimental import pallas as pl
from jax.experimental.pallas import tpu as pltpu

F32 = jnp.float32
BF16 = jnp.bfloat16

D_MODEL = 2048
DEPTH = 4
PAGE_SIZE = 128
BRANCH_W = D_MODEL // 2
A_HD = 64
A_HEADS = BRANCH_W // (2 * A_HD)
A_DV = 2 * A_HD
LRU_W = BRANCH_W
LRU_BLOCKS = 16
LRU_BW = LRU_W // LRU_BLOCKS
CONV_W = 4
LRU_C = 8.0
C_W = BRANCH_W
C_GROUPS = 4
C_GW = C_W // C_GROUPS
CHUNK = 128
N_BRANCH = 3
FF_W = 4 * D_MODEL
EPS = 1e-6

COL_Q = 0
COL_K = BRANCH_W
COL_V = 2 * BRANCH_W
COL_XB = 3 * BRANCH_W
COL_Z = 4 * BRANCH_W
COL_G = 6 * BRANCH_W
W_IN = COL_G + N_BRANCH * D_MODEL

LANES = 128
SUBLANES = 8
NEG_BIG = -1e30
LOG2E = math.log2(math.e)
MIB = 1024 * 1024
VMEM_LIMIT = 52 * MIB


def _params(semantics):
    return pltpu.CompilerParams(dimension_semantics=semantics, vmem_limit_bytes=VMEM_LIMIT)


def _rms(x, g):
    return x * lax.rsqrt(jnp.mean(x * x, axis=-1, keepdims=True) + EPS) * g


def _norm_mm_kernel(x_ref, g_ref, w_ref, o_ref, xn_ref):
    @pl.when(pl.program_id(1) == 0)
    def _():
        xn_ref[...] = _rms(x_ref[...], g_ref[...]).astype(BF16)

    o_ref[...] = jnp.dot(xn_ref[...], w_ref[...].astype(BF16), preferred_element_type=F32)


def norm_matmul(x, g3, w3, layer, *, tm, tn):
    m, k = x.shape
    n = w3.shape[-1]
    return pl.pallas_call(
        _norm_mm_kernel,
        grid=(m // tm, n // tn),
        in_specs=[
            pl.BlockSpec((tm, k), lambda i, j: (i, 0)),
            pl.BlockSpec((None, 1, k), lambda i, j: (layer, 0, 0)),
            pl.BlockSpec((None, k, tn), lambda i, j: (layer, 0, j)),
        ],
        out_specs=pl.BlockSpec((tm, tn), lambda i, j: (i, j)),
        out_shape=jax.ShapeDtypeStruct((m, n), F32),
        scratch_shapes=[pltpu.VMEM((tm, k), BF16)],
        compiler_params=_params(("parallel", "arbitrary")),
        name="in_proj",
    )(x, g3, w3)


def _lambda_value(lq, lam_init):
    a = jnp.sum(lq[0:1] * lq[1:2], axis=-1, keepdims=True)
    b = jnp.sum(lq[2:3] * lq[3:4], axis=-1, keepdims=True)
    return jnp.exp(a) - jnp.exp(b) + lam_init


def _online_softmax_update(s, off, vblk, m_ref, l_ref, acc_ref):
    m_old = m_ref[...]
    m_new = jnp.maximum(m_old, jnp.max(s, axis=-1, keepdims=True) + off)
    p = jnp.exp2(s - (m_new - off))
    alpha = jnp.exp2(m_old - m_new)
    l_ref[...] = alpha * l_ref[...] + jnp.sum(p, axis=-1, keepdims=True)
    acc_ref[...] = alpha * acc_ref[...] + jnp.dot(
        p.astype(BF16), vblk, preferred_element_type=F32)
    m_ref[...] = m_new


_NT_DIMS = (((1,), (1,)), ((), ()))


BIAS_SPLIT = 3
ONES_ROWS = 16
HEADS_PER_STEP = 4


def _bf16_pieces(x):
    out = []
    for _ in range(BIAS_SPLIT):
        hi = x.astype(BF16).astype(F32)
        out.append(hi)
        x = x - hi
    return out


def _lane_select(lane, columns):
    out = jnp.zeros_like(columns[0])
    for i, col in enumerate(columns):
        out = jnp.where(lane == i, col, out)
    return out


def _attn_prompt_kernel(slopes_ref, q_ref, k_ref, v_ref, lq_ref, sg_ref, o_ref,
                        kb_ref, vt_ref, qe_ref, mask_ref, m_ref, acc_ref,
                        *, tq, lam_init):
    hp = pl.program_id(1)
    qi = pl.program_id(2)
    n_blk = vt_ref.shape[1]
    seq = kb_ref.shape[1]
    slopes2 = [slopes_ref[hp * HEADS_PER_STEP + g] * LOG2E for g in range(HEADS_PER_STEP)]

    @pl.when(qi == 0)
    def _():
        lane = lax.broadcasted_iota(jnp.int32, (seq, A_DV), 1)
        kidx = (lax.broadcasted_iota(jnp.int32, (seq, A_DV), 0) & (tq - 1)).astype(F32)
        kside = jnp.where(lane < BIAS_SPLIT, kidx, jnp.where(lane < 2 * BIAS_SPLIT, 1.0, 0.0))
        qlane = lax.broadcasted_iota(jnp.int32, (2 * tq, A_DV), 1)
        qidx = (lax.broadcasted_iota(jnp.int32, (2 * tq, A_DV), 0) & (tq - 1)).astype(F32)
        for g in range(HEADS_PER_STEP):
            cs = slice(g * A_DV, (g + 1) * A_DV)
            kb_ref[g, :, :A_DV] = k_ref[:, cs].astype(BF16)
            kb_ref[g, :, A_DV:] = kside.astype(BF16)
            for jb in range(n_blk):
                vt_ref[g, jb, :A_DV, :] = v_ref[jb * tq:(jb + 1) * tq, cs].T.astype(BF16)
                vt_ref[g, jb, A_DV:, :] = jnp.ones((ONES_ROWS, tq), BF16)
            sl = jnp.full((2 * tq, A_DV), slopes2[g], F32)
            cols = _bf16_pieces(sl) + _bf16_pieces(-(sl * qidx))
            qe_ref[g] = _lane_select(qlane, cols).astype(BF16)
        r = lax.broadcasted_iota(jnp.int32, (tq, 2 * tq), 0)
        c = lax.broadcasted_iota(jnp.int32, (tq, 2 * tq), 1) & (tq - 1)
        mask_ref[...] = jnp.where(r <= c, 0.0, NEG_BIG)

    lane = lax.broadcasted_iota(jnp.int32, (tq, A_DV), 1)
    qs = []
    for g in range(HEADS_PER_STEP):
        q = q_ref[:, g * A_DV:(g + 1) * A_DV] * (A_HD ** -0.5 * LOG2E)
        q2 = jnp.concatenate(
            [jnp.where(lane < A_HD, q, 0.0), jnp.where(lane >= A_HD, q, 0.0)], axis=0)
        qs.append(jnp.concatenate([q2.astype(BF16), qe_ref[g]], axis=1))
    m_ref[...] = jnp.full(m_ref.shape, NEG_BIG, F32)
    acc_ref[...] = jnp.zeros(acc_ref.shape, F32)

    def step(g, j, off, mask):
        start = pl.multiple_of(j * tq, tq)
        kblk = kb_ref[g, pl.ds(start, tq), :]
        s = lax.dot_general(kblk, qs[g], _NT_DIMS, preferred_element_type=F32)
        if mask is not None:
            s = s + mask
        m_old = m_ref[g]
        m_new = jnp.maximum(m_old, jnp.max(s, axis=0, keepdims=True) + off)
        p = jnp.exp2(s - (m_new - off))
        alpha = jnp.exp2(m_old - m_new)
        acc_ref[g] = alpha * acc_ref[g] + jnp.dot(
            vt_ref[g, j], p.astype(BF16), preferred_element_type=F32)
        m_ref[g] = m_new

    def body(j, carry):
        rel = ((j - qi) * tq).astype(F32)
        for g in range(HEADS_PER_STEP):
            step(g, j, slopes2[g] * rel, None)
        return carry

    lax.fori_loop(0, qi, body, 0)
    mask = mask_ref[...]
    for g in range(HEADS_PER_STEP):
        step(g, qi, 0.0, mask)

    lam = _lambda_value(lq_ref[...], lam_init)
    for g in range(HEADS_PER_STEP):
        acc = acc_ref[g]
        on = acc[:A_DV] / acc[A_DV:A_DV + 1]
        ot = on[:, :tq] - lam * on[:, tq:]
        yt = ot * lax.rsqrt(jnp.mean(ot * ot, axis=0, keepdims=True) + EPS)
        o_ref[:, g * A_DV:(g + 1) * A_DV] = (
            (yt.T * sg_ref[...]) * (1.0 - lam_init)).astype(BF16)


def attention_prompt(p3, slopes, lam_qk, subln_g3, layer, lam_init, *, tq):
    b, s, _ = p3.shape
    gw = HEADS_PER_STEP * A_DV
    kern = functools.partial(_attn_prompt_kernel, tq=tq, lam_init=lam_init)
    return pl.pallas_call(
        kern,
        grid=(b, A_HEADS // HEADS_PER_STEP, s // tq),
        in_specs=[
            pl.BlockSpec(memory_space=pltpu.SMEM),
            pl.BlockSpec((None, tq, gw), lambda bi, h, qi: (bi, qi, COL_Q // gw + h)),
            pl.BlockSpec((None, s, gw), lambda bi, h, qi: (bi, 0, COL_K // gw + h)),
            pl.BlockSpec((None, s, gw), lambda bi, h, qi: (bi, 0, COL_V // gw + h)),
            pl.BlockSpec((None, 4, A_HD), lambda bi, h, qi: (layer, 0, 0)),
            pl.BlockSpec((None, 1, A_DV), lambda bi, h, qi: (layer, 0, 0)),
        ],
        out_specs=pl.BlockSpec((None, tq, gw), lambda bi, h, qi: (bi, qi, h)),
        out_shape=jax.ShapeDtypeStruct((b, s, BRANCH_W), BF16),
        scratch_shapes=[
            pltpu.VMEM((HEADS_PER_STEP, s, 2 * A_DV), BF16),
            pltpu.VMEM((HEADS_PER_STEP, s // tq, A_DV + ONES_ROWS, tq), BF16),
            pltpu.VMEM((HEADS_PER_STEP, 2 * tq, A_DV), BF16),
            pltpu.VMEM((tq, 2 * tq), F32),
            pltpu.VMEM((HEADS_PER_STEP, 1, 2 * tq), F32),
            pltpu.VMEM((HEADS_PER_STEP, A_DV + ONES_ROWS, 2 * tq), F32),
        ],
        compiler_params=_params(("parallel", "arbitrary", "arbitrary")),
        name="attn_prompt",
    )(slopes, p3, p3, p3, lam_qk, subln_g3)


def _attn_sample_kernel(pt_ref, q_ref, kn_ref, vn_ref, slope_ref, lq_ref, sg_ref, *rest,
                        n_pages, n_steps, past_len, lam_init):
    del pt_ref
    kp_refs = rest[:n_pages]
    vp_refs = rest[n_pages:2 * n_pages]
    o_ref, qall_ref, sb_ref, m_ref, l_ref, acc_ref = rest[2 * n_pages:]
    s_idx = pl.program_id(1)
    n_rows = 2 * A_HEADS * SUBLANES
    half = A_HEADS * SUBLANES
    span = n_pages * PAGE_SIZE
    slope = slope_ref[...] * LOG2E

    def head_cols(ref, h):
        return ref[:, h * A_DV:(h + 1) * A_DV]

    @pl.when(s_idx == 0)
    def _():
        lane = lax.broadcasted_iota(jnp.int32, (SUBLANES, A_DV), 1)
        pieces = []
        for mp in range(2):
            keep = (lane >= A_HD) if mp else (lane < A_HD)
            for h in range(A_HEADS):
                pieces.append(
                    jnp.where(keep, head_cols(q_ref, h) * (A_HD ** -0.5 * LOG2E), 0.0))
        qall_ref[...] = jnp.concatenate(pieces, axis=0).astype(BF16)
        r = lax.broadcasted_iota(jnp.int32, sb_ref.shape, 0)
        c = lax.broadcasted_iota(jnp.int32, sb_ref.shape, 1)
        bias = slope * ((c >> 3) - (r & 7)).astype(F32)
        sb_ref[...] = jnp.where((c & 7) == ((r >> 3) & 7), bias, NEG_BIG)
        m_ref[...] = jnp.full(m_ref.shape, NEG_BIG, F32)
        l_ref[...] = jnp.zeros(l_ref.shape, F32)
        acc_ref[...] = jnp.zeros(acc_ref.shape, F32)

    qall = qall_ref[...]
    kcat = jnp.concatenate([r[...].astype(BF16) for r in kp_refs], axis=0)
    vcat = jnp.concatenate([r[...].astype(BF16) for r in vp_refs], axis=0)
    s = lax.dot_general(qall, kcat, _NT_DIMS, preferred_element_type=F32) + sb_ref[...]
    off = slope * (s_idx * span - past_len).astype(F32)
    _online_softmax_update(s, off, vcat, m_ref, l_ref, acc_ref)

    @pl.when(s_idx == n_steps - 1)
    def _():
        pad = jnp.zeros((PAGE_SIZE - half, A_DV), F32)
        kn = jnp.concatenate([head_cols(kn_ref, h) for h in range(A_HEADS)] + [pad], axis=0)
        vn = jnp.concatenate([head_cols(vn_ref, h) for h in range(A_HEADS)] + [pad], axis=0)
        sn = lax.dot_general(qall, kn.astype(BF16), _NT_DIMS, preferred_element_type=F32)
        r = lax.broadcasted_iota(jnp.int32, sn.shape, 0)
        c = lax.broadcasted_iota(jnp.int32, sn.shape, 1)
        qpos = r & 7
        kpos = c & 7
        valid = (c < half) & ((c >> 3) == ((r >> 3) & 7)) & (kpos <= qpos)
        sn = jnp.where(valid, sn + slope * (kpos - qpos).astype(F32), NEG_BIG)
        _online_softmax_update(sn, 0.0, vn.astype(BF16), m_ref, l_ref, acc_ref)

        o_map = acc_ref[...] / l_ref[...]
        lam = _lambda_value(lq_ref[...], lam_init)
        outs = []
        for h in range(A_HEADS):
            o = (o_map[h * SUBLANES:(h + 1) * SUBLANES]
                 - lam * o_map[half + h * SUBLANES:half + (h + 1) * SUBLANES])
            outs.append(_rms(o, sg_ref[...]) * (1.0 - lam_init))
        o_ref[...] = jnp.concatenate(outs, axis=1).astype(BF16)


def attention_sample(p3, cache_k4, cache_v4, page_table, slope_rows, lam_qk, subln_g3,
                     layer, lam_init, *, n_pages):
    b, t, _ = p3.shape
    assert t == SUBLANES
    pages_total = page_table.shape[1]
    n_steps = pages_total // n_pages
    n_rows = 2 * A_HEADS * SUBLANES
    page_rows = PAGE_SIZE * A_HEADS
    kern = functools.partial(
        _attn_sample_kernel, n_pages=n_pages, n_steps=n_steps,
        past_len=pages_total * PAGE_SIZE, lam_init=lam_init)

    def page_spec(i):
        return pl.BlockSpec(
            (None, None, page_rows, A_DV),
            lambda bi, si, pt: (layer, pt[bi, si * n_pages + i], 0, 0))

    col = lambda c: pl.BlockSpec((None, t, BRANCH_W), lambda bi, si, pt: (bi, 0, c // BRANCH_W))
    grid_spec = pltpu.PrefetchScalarGridSpec(
        num_scalar_prefetch=1,
        grid=(b, n_steps),
        in_specs=[
            col(COL_Q), col(COL_K), col(COL_V),
            pl.BlockSpec((n_rows, 1), lambda bi, si, pt: (0, 0)),
            pl.BlockSpec((None, 4, A_HD), lambda bi, si, pt: (layer, 0, 0)),
            pl.BlockSpec((None, 1, A_DV), lambda bi, si, pt: (layer, 0, 0)),
        ] + [page_spec(i) for i in range(n_pages)] * 2,
        out_specs=pl.BlockSpec((None, t, BRANCH_W), lambda bi, si, pt: (bi, 0, 0)),
        scratch_shapes=[
            pltpu.VMEM((n_rows, A_DV), BF16),
            pltpu.VMEM((n_rows, n_pages * page_rows), F32),
            pltpu.VMEM((n_rows, 1), F32),
            pltpu.VMEM((n_rows, 1), F32),
            pltpu.VMEM((n_rows, A_DV), F32),
        ],
    )
    return pl.pallas_call(
        kern,
        grid_spec=grid_spec,
        out_shape=jax.ShapeDtypeStruct((b, t, BRANCH_W), BF16),
        compiler_params=_params(("parallel", "arbitrary")),
        name="attn_sample",
    )(page_table, p3, p3, p3, slope_rows, lam_qk, subln_g3,
      *([cache_k4] * n_pages), *([cache_v4] * n_pages))


def _shift_rows(x, prev, j, row8):
    xr = pltpu.roll(x, j, 0)
    head = jnp.where(row8 < j, pltpu.roll(prev, j, 0), xr[:SUBLANES])
    if x.shape[0] == SUBLANES:
        return head
    return jnp.concatenate([head, xr[SUBLANES:]], axis=0)


def _lru_kernel(x_ref, c0_ref, h0_ref, cw_ref, cb_ref, wr_ref, wi_ref, br_ref, bi_ref, lam_ref,
                hs_ref, hl_ref, co_ref, hprev_ref, xprev_ref):
    t_idx = pl.program_id(2)

    @pl.when(t_idx == 0)
    def _():
        hprev_ref[...] = h0_ref[...]
        xprev_ref[...] = c0_ref[...]

    x = x_ref[...]
    n_t, tc = x.shape
    prev = xprev_ref[...]
    row8 = lax.broadcasted_iota(jnp.int32, (SUBLANES, tc), 0)
    cw = cw_ref[...]
    xc = cb_ref[...] + cw[0:1] * _shift_rows(x, prev, 3, row8)
    xc = xc + cw[1:2] * _shift_rows(x, prev, 2, row8)
    xc = xc + cw[2:3] * _shift_rows(x, prev, 1, row8)
    xc = xc + cw[3:4] * x
    tail = x[n_t - SUBLANES:]
    xprev_ref[...] = tail
    co_ref[...] = tail

    xcb = xc.astype(BF16)
    r = jax.nn.sigmoid(jnp.dot(xcb, wr_ref[...], preferred_element_type=F32) + br_ref[...])
    i = jax.nn.sigmoid(jnp.dot(xcb, wi_ref[...], preferred_element_type=F32) + bi_ref[...])
    nl = -lam_ref[...]
    softplus = jnp.maximum(nl, 0.0) + jnp.log1p(jnp.exp(-jnp.abs(nl)))
    log_a = (-LRU_C) * r * softplus
    a = jnp.exp(log_a)
    th = jnp.tanh(log_a)
    u = jnp.sqrt(-2.0 * th / (1.0 - th)) * (i * xc)

    row = lax.broadcasted_iota(jnp.int32, (n_t, tc), 0)
    d = 1
    while d < n_t:
        if d < SUBLANES:
            a_sh = jnp.where(row >= d, pltpu.roll(a, d, 0), 1.0)
            u_sh = jnp.where(row >= d, pltpu.roll(u, d, 0), 0.0)
        else:
            a_sh = jnp.concatenate([jnp.ones((d, tc), F32), a[:n_t - d]], axis=0)
            u_sh = jnp.concatenate([jnp.zeros((d, tc), F32), u[:n_t - d]], axis=0)
        u = u + a * u_sh
        a = a * a_sh
        d *= 2
    hseq = u + a * hprev_ref[...]
    hs_ref[...] = hseq.astype(BF16)
    last = hseq[n_t - 1:]
    hprev_ref[...] = last
    hl_ref[...] = last


def rg_lru(p3, conv0, h0, conv_w, conv_b3, wr_bd, wi_bd, b_rg3, b_ig3, lam3, layer, *, tt, tc):
    b, s, _ = p3.shape
    c = LRU_W
    vec = lambda: pl.BlockSpec((None, 1, tc), lambda bi, ci, ti: (layer, 0, ci))
    gate_w = lambda: pl.BlockSpec((None, None, tc, tc), lambda bi, ci, ti: (layer, ci, 0, 0))
    state = lambda rows: pl.BlockSpec((None, rows, tc), lambda bi, ci, ti: (bi, 0, ci))
    return pl.pallas_call(
        _lru_kernel,
        grid=(b, c // tc, s // tt),
        in_specs=[
            pl.BlockSpec((None, tt, tc), lambda bi, ci, ti: (bi, ti, COL_XB // tc + ci)),
            state(SUBLANES), state(1),
            pl.BlockSpec((None, CONV_W, tc), lambda bi, ci, ti: (layer, 0, ci)),
            vec(), gate_w(), gate_w(), vec(), vec(), vec(),
        ],
        out_specs=[
            pl.BlockSpec((None, tt, tc), lambda bi, ci, ti: (bi, ti, ci)),
            state(1), state(SUBLANES),
        ],
        out_shape=[
            jax.ShapeDtypeStruct((b, s, c), BF16),
            jax.ShapeDtypeStruct((b, 1, c), F32),
            jax.ShapeDtypeStruct((b, SUBLANES, c), F32),
        ],
        scratch_shapes=[pltpu.VMEM((1, tc), F32), pltpu.VMEM((SUBLANES, tc), F32)],
        compiler_params=_params(("parallel", "parallel", "arbitrary")),
        name="rg_lru",
    )(p3, conv0, h0, conv_w, conv_b3, wr_bd, wi_bd, b_rg3, b_ig3, lam3)


def _gelu_tanh(x):
    c = math.sqrt(2.0 / math.pi)
    return 0.5 * x * (1.0 + jnp.tanh(c * (x + 0.044715 * (x * x * x))))


def _cmlp_kernel(zu_ref, zv_ref, lg_ref, lb_ref, ws_ref, bst_ref, oc_ref, *maybe_vc_ref, ch):
    u = _gelu_tanh(zu_ref[...])
    vg = _gelu_tanh(zv_ref[...])
    xc = vg - jnp.mean(vg, axis=-1, keepdims=True)
    v = xc * lax.rsqrt(jnp.mean(xc * xc, axis=-1, keepdims=True) + EPS) * lg_ref[...] + lb_ref[...]
    if maybe_vc_ref:
        maybe_vc_ref[0][...] = v
    n_ch = u.shape[0] // ch
    r = lax.broadcasted_iota(jnp.int32, (ch, ch), 0)
    c = lax.broadcasted_iota(jnp.int32, (ch, ch), 1)
    for g in range(C_GROUPS):
        wm = jnp.where(c <= r, ws_ref[g][:ch, :ch], 0.0)
        bias = bst_ref[:ch, g:g + 1]
        cs = slice(g * C_GW, (g + 1) * C_GW)
        for n in range(n_ch):
            rs = slice(n * ch, (n + 1) * ch)
            vb = v[rs, cs]
            if ch >= LANES:
                f = jnp.dot(wm.astype(BF16), vb.astype(BF16), preferred_element_type=F32) + bias
            else:
                f = bias + wm[:, 0:1] * vb[0:1, :]
                for s in range(1, ch):
                    f = f + wm[:, s:s + 1] * vb[s:s + 1, :]
            oc_ref[rs, cs] = (u[rs, cs] * f).astype(BF16)


def chunk_mlp(p, ln_g3, ln_b3, w_s, b_st, layer, *, tm, ch, emit_v):
    m = p.shape[0]
    vec = lambda: pl.BlockSpec((None, 1, C_W), lambda i: (layer, 0, 0))
    out_specs = [pl.BlockSpec((tm, C_W), lambda i: (i, 0))]
    out_shape = [jax.ShapeDtypeStruct((m, C_W), BF16)]
    if emit_v:
        out_specs.append(pl.BlockSpec((tm, C_W), lambda i: (i, 0)))
        out_shape.append(jax.ShapeDtypeStruct((m, C_W), F32))
    return pl.pallas_call(
        functools.partial(_cmlp_kernel, ch=ch),
        grid=(m // tm,),
        in_specs=[
            pl.BlockSpec((tm, C_W), lambda i: (i, COL_Z // C_W)),
            pl.BlockSpec((tm, C_W), lambda i: (i, COL_Z // C_W + 1)),
            vec(), vec(),
            pl.BlockSpec((None, C_GROUPS, CHUNK, CHUNK), lambda i: (layer, 0, 0, 0)),
            pl.BlockSpec((None, CHUNK, C_GROUPS), lambda i: (layer, 0, 0)),
        ],
        out_specs=out_specs,
        out_shape=out_shape,
        compiler_params=_params(("parallel",)),
        name="chunk_mlp",
    )(p, p, ln_g3, ln_b3, w_s, b_st)


def _branch_kernel(oa_ref, ob_ref, oc_ref, g0_ref, g1_ref, g2_ref, wb_ref, m_ref):
    acc = None
    for j, (o_ref, g_ref) in enumerate(((oa_ref, g0_ref), (ob_ref, g1_ref), (oc_ref, g2_ref))):
        y = jnp.dot(o_ref[...], wb_ref[j].astype(BF16), preferred_element_type=F32)
        t = jax.nn.sigmoid(g_ref[...]) * y
        acc = t if acc is None else acc + t
    m_ref[...] = acc.astype(BF16)


def branch_merge(o_a, o_b, o_c, p, wb, layer, *, tm, tn):
    m = o_a.shape[0]
    o_spec = lambda: pl.BlockSpec((tm, BRANCH_W), lambda i, j: (i, 0))
    g_spec = lambda b: pl.BlockSpec((tm, tn), lambda i, j: (i, (COL_G + b * D_MODEL) // tn + j))
    return pl.pallas_call(
        _branch_kernel,
        grid=(m // tm, D_MODEL // tn),
        in_specs=[o_spec(), o_spec(), o_spec(), g_spec(0), g_spec(1), g_spec(2),
                  pl.BlockSpec((None, N_BRANCH, BRANCH_W, tn), lambda i, j: (layer, 0, 0, j))],
        out_specs=pl.BlockSpec((tm, tn), lambda i, j: (i, j)),
        out_shape=jax.ShapeDtypeStruct((m, D_MODEL), BF16),
        compiler_params=_params(("parallel", "arbitrary")),
        name="branch_merge",
    )(o_a, o_b, o_c, p, p, p, wb)


def _mm_res_kernel(x_ref, w_ref, h_ref, o_ref):
    o_ref[...] = h_ref[...] + jnp.dot(
        x_ref[...], w_ref[...].astype(BF16), preferred_element_type=F32)


def matmul_residual(x, w3, h, layer, *, tm, tn):
    m, k = x.shape
    n = w3.shape[-1]
    return pl.pallas_call(
        _mm_res_kernel,
        grid=(m // tm, n // tn),
        in_specs=[
            pl.BlockSpec((tm, k), lambda i, j: (i, 0)),
            pl.BlockSpec((None, k, tn), lambda i, j: (layer, 0, j)),
            pl.BlockSpec((tm, tn), lambda i, j: (i, j)),
        ],
        out_specs=pl.BlockSpec((tm, tn), lambda i, j: (i, j)),
        out_shape=jax.ShapeDtypeStruct((m, n), F32),
        compiler_params=_params(("parallel", "arbitrary")),
        name="out_proj",
    )(x, w3, h)


def _ffn_kernel(h_ref, g_ref, w1_ref, w2_ref, o_ref, xn_ref):
    @pl.when(pl.program_id(1) == 0)
    def _():
        h = h_ref[...]
        xn_ref[...] = _rms(h, g_ref[...]).astype(BF16)
        o_ref[...] = h

    hid = jnp.maximum(jnp.dot(xn_ref[...], w1_ref[...], preferred_element_type=F32), 0.0)
    o_ref[...] += jnp.dot((hid * hid).astype(BF16), w2_ref[...], preferred_element_type=F32)


def ffn(h, g3, w1, w2, layer, *, tm, tf):
    m, d = h.shape
    return pl.pallas_call(
        _ffn_kernel,
        grid=(m // tm, FF_W // tf),
        in_specs=[
            pl.BlockSpec((tm, d), lambda i, j: (i, 0)),
            pl.BlockSpec((None, 1, d), lambda i, j: (layer, 0, 0)),
            pl.BlockSpec((None, d, tf), lambda i, j: (layer, 0, j)),
            pl.BlockSpec((None, tf, d), lambda i, j: (layer, j, 0)),
        ],
        out_specs=pl.BlockSpec((tm, d), lambda i, j: (i, 0)),
        out_shape=jax.ShapeDtypeStruct((m, d), F32),
        scratch_shapes=[pltpu.VMEM((tm, d), BF16)],
        compiler_params=_params(("parallel", "arbitrary")),
        name="ffn",
    )(h, g3, w1, w2)


def _final_norm_kernel(x_ref, g_ref, o_ref):
    o_ref[...] = _rms(x_ref[...], g_ref[...])


def final_norm(x, g2, *, tm):
    m, d = x.shape
    return pl.pallas_call(
        _final_norm_kernel,
        grid=(m // tm,),
        in_specs=[pl.BlockSpec((tm, d), lambda i: (i, 0)), pl.BlockSpec((1, d), lambda i: (0, 0))],
        out_specs=pl.BlockSpec((tm, d), lambda i: (i, 0)),
        out_shape=jax.ShapeDtypeStruct((m, d), F32),
        compiler_params=_params(("parallel",)),
        name="final_norm",
    )(x, g2)


def _block_diag_gates(w):
    per = 256 // LRU_BW
    w5 = w.reshape(DEPTH, LRU_BLOCKS // per, per, LRU_BW, LRU_BW)
    eye = jnp.eye(per, dtype=w.dtype)
    bd = jnp.einsum("lgbij,bc->lgbicj", w5, eye)
    return bd.reshape(DEPTH, LRU_BLOCKS // per, 256, 256).astype(BF16)


def _trunk_layer(h, layer, wts, attend, conv0, h0, *, tiles, ch, emit_v):
    bsz, seq, _ = h.shape
    m = bsz * seq
    h2 = h.reshape(m, D_MODEL)
    p = norm_matmul(h2, wts["norm1_g"], wts["w_in"], layer, tm=tiles["tm"], tn=tiles["tn_in"])
    p3 = p.reshape(bsz, seq, W_IN)
    o_a = attend(p3).reshape(m, BRANCH_W)
    o_b, h_last, conv_tail = rg_lru(
        p3, conv0, h0, wts["conv_w"], wts["conv_b"], wts["w_rg"], wts["w_ig"],
        wts["b_rg"], wts["b_ig"], wts["lru_lambda"], layer, tt=tiles["tt"], tc=256)
    cm = chunk_mlp(p, wts["cmlp_ln_g"], wts["cmlp_ln_b"], wts["w_s"], wts["b_st"], layer,
                   tm=tiles["tm_c"], ch=ch, emit_v=emit_v)
    mrg = branch_merge(o_a, o_b.reshape(m, BRANCH_W), cm[0], p, wts["w_branch"], layer,
                       tm=tiles["tm"], tn=tiles["tn"])
    h2 = matmul_residual(mrg, wts["w_out"], h2, layer, tm=tiles["tm"], tn=tiles["tn"])
    h2 = ffn(h2, wts["norm2_g"], wts["w_ff1"], wts["w_ff2"], layer, tm=tiles["tm_f"], tf=tiles["tf"])
    k = p3[..., COL_K:COL_V].reshape(bsz, seq, A_HEADS, 2 * A_HD)
    v = p3[..., COL_V:COL_XB].reshape(bsz, seq, A_HEADS, A_DV)
    new_buf = conv_tail[:, SUBLANES - (CONV_W - 1):]
    v_c = cm[1].reshape(bsz, seq, C_W) if emit_v else None
    return h2.reshape(bsz, seq, D_MODEL), k, v, new_buf, h_last[:, 0], v_c


def kernel(x_prompt, x_sample, cache_k, cache_v, state_h, state_conv, page_table,
           norm1_g, w_in, lam_qk, subln_g, conv_w, conv_b, w_rg, b_rg, w_ig, b_ig, lru_lambda,
           cmlp_ln_g, cmlp_ln_b, w_s, b_s, w_branch, w_out, norm2_g, w_ff1, w_ff2, final_g):
    bp, sp, _ = x_prompt.shape
    bs, ss, _ = x_sample.shape
    n_pool = cache_k.shape[1]
    row = lambda a: a.reshape(DEPTH, 1, a.shape[-1])
    wts = {
        "norm1_g": row(norm1_g), "w_in": w_in,
        "conv_w": conv_w, "conv_b": row(conv_b),
        "w_rg": _block_diag_gates(w_rg), "w_ig": _block_diag_gates(w_ig),
        "b_rg": row(b_rg), "b_ig": row(b_ig), "lru_lambda": row(lru_lambda),
        "cmlp_ln_g": row(cmlp_ln_g), "cmlp_ln_b": row(cmlp_ln_b),
        "w_s": w_s, "b_st": jnp.swapaxes(b_s, 1, 2),
        "w_branch": w_branch, "w_out": w_out,
        "norm2_g": row(norm2_g), "w_ff1": w_ff1.astype(BF16), "w_ff2": w_ff2.astype(BF16),
    }
    subln_g3 = row(subln_g)
    slopes = jnp.exp2(-8.0 * jnp.arange(1, A_HEADS + 1, dtype=F32) / A_HEADS)
    slope_rows = jnp.tile(jnp.repeat(slopes, SUBLANES), 2).reshape(2 * A_HEADS * SUBLANES, 1)
    cache_k4 = cache_k.reshape(DEPTH, n_pool, PAGE_SIZE * A_HEADS, A_DV)
    cache_v4 = cache_v.reshape(DEPTH, n_pool, PAGE_SIZE * A_HEADS, A_DV)
    conv0_p = jnp.zeros((bp, SUBLANES, LRU_W), F32)
    h0_p = jnp.zeros((bp, 1, LRU_W), F32)
    conv0_s = jnp.pad(state_conv, ((0, 0), (0, 0), (SUBLANES - (CONV_W - 1), 0), (0, 0)))

    tiles_p = dict(tm=1024, tn_in=512, tn=512, tt=256, tm_c=512, tm_f=512, tf=1024)
    ms = bs * ss
    tiles_s = dict(tm=ms, tn_in=512, tn=512, tt=ss, tm_c=ms, tm_f=ms, tf=1024)

    hp, hs = x_prompt, x_sample
    outs = [[] for _ in range(9)]
    for l in range(DEPTH):
        lam_init = 0.8 - 0.6 * math.exp(-0.3 * l)
        attend_p = functools.partial(
            attention_prompt, slopes=slopes, lam_qk=lam_qk, subln_g3=subln_g3,
            layer=l, lam_init=lam_init, tq=256)
        hp, kp, vp, cp, hlp, _ = _trunk_layer(
            hp, l, wts, attend_p, conv0_p, h0_p, tiles=tiles_p, ch=CHUNK, emit_v=False)
        attend_s = functools.partial(
            attention_sample, cache_k4=cache_k4, cache_v4=cache_v4, page_table=page_table,
            slope_rows=slope_rows, lam_qk=lam_qk, subln_g3=subln_g3,
            layer=l, lam_init=lam_init, n_pages=4)
        hs, k_s, v_s, c_s, h_s, vc_s = _trunk_layer(
            hs, l, wts, attend_s, conv0_s[l], state_h[l][:, None, :],
            tiles=tiles_s, ch=min(ss, CHUNK), emit_v=True)
        for lst, val in zip(outs, (kp, vp, hlp, cp, k_s, v_s, h_s, c_s, vc_s)):
            lst.append(val)
    y_prompt = final_norm(hp.reshape(bp * sp, D_MODEL), final_g.reshape(1, D_MODEL), tm=1024)
    y_sample = final_norm(hs.reshape(ms, D_MODEL), final_g.reshape(1, D_MODEL), tm=ms)
    stacked = [jnp.stack(lst) for lst in outs]
    return (y_prompt.reshape(bp, sp, D_MODEL), y_sample.reshape(bs, ss, D_MODEL), *stacked)
```

```python
import functools
import math

import jax
import jax.numpy as jnp
from jax import lax
from jax.experimental import pallas as pl
from jax.experimental.pallas import tpu as pltpu

F32 = jnp.float32
BF16 = jnp.bfloat16

D_MODEL = 2048
DEPTH = 4
PAGE_SIZE = 128
BRANCH_W = D_MODEL // 2
A_HD = 64
A_HEADS = BRANCH_W // (2 * A_HD)
A_DV = 2 * A_HD
LRU_W = BRANCH_W
LRU_BLOCKS = 16
LRU_BW = LRU_W // LRU_BLOCKS
CONV_W = 4
LRU_C = 8.0
C_W = BRANCH_W
C_GROUPS = 4
C_GW = C_W // C_GROUPS
CHUNK = 128
N_BRANCH = 3
FF_W = 4 * D_MODEL
EPS = 1e-6

COL_Q = 0
COL_K = BRANCH_W
COL_V = 2 * BRANCH_W
COL_XB = 3 * BRANCH_W
COL_Z = 4 * BRANCH_W
COL_G = 6 * BRANCH_W
W_IN = COL_G + N_BRANCH * D_MODEL

LANES = 128
SUBLANES = 8
NEG_BIG = -1e30
LOG2E = math.log2(math.e)
MIB = 1024 * 1024
VMEM_LIMIT = 52 * MIB


def _params(semantics):
    return pltpu.CompilerParams(dimension_semantics=semantics, vmem_limit_bytes=VMEM_LIMIT)


def _rms(x, g):
    return x * lax.rsqrt(jnp.mean(x * x, axis=-1, keepdims=True) + EPS) * g


def _norm_mm_kernel(x_ref, g_ref, w_ref, *rest, tn, n_passthrough):
    rest = rest[n_passthrough:]
    o_ref, xn_ref = rest[0], rest[-1]
    cache_refs = rest[1:-1]
    j = pl.program_id(1)

    @pl.when(j == 0)
    def _():
        xn_ref[...] = _rms(x_ref[...], g_ref[...]).astype(BF16)

    o_ref[...] = jnp.dot(xn_ref[...], w_ref[...], preferred_element_type=F32)
    tm = o_ref.shape[0]
    heads_per_tile = tn // A_DV
    for dst, col0 in zip(cache_refs, (COL_K, COL_V)):
        for t in range(BRANCH_W // tn):
            @pl.when(j == col0 // tn + t)
            def _(dst=dst, t=t):
                for hh in range(heads_per_tile):
                    h = t * heads_per_tile + hh
                    dst[pl.ds(h, tm, stride=A_HEADS), :] = o_ref[:, hh * A_DV:(hh + 1) * A_DV]


def norm_matmul(x, g3, w3, layer, *, tm, tn, caches=None, emit_cache=False):
    m, k = x.shape
    n = w3.shape[-1]
    in_specs = [
        pl.BlockSpec((tm, k), lambda i, j: (i, 0)),
        pl.BlockSpec((None, 1, k), lambda i, j: (layer, 0, 0)),
        pl.BlockSpec((None, k, tn), lambda i, j: (layer, 0, j)),
    ]
    out_specs = [pl.BlockSpec((tm, tn), lambda i, j: (i, j))]
    out_shape = [jax.ShapeDtypeStruct((m, n), F32)]
    args = [x, g3, w3]
    aliases = {}
    if emit_cache:
        for c in range(2):
            out_specs.append(pl.BlockSpec((None, tm * A_HEADS, A_DV), lambda i, j: (layer, i, 0)))
            out_shape.append(jax.ShapeDtypeStruct((DEPTH, m * A_HEADS, A_DV), F32))
            if caches is not None:
                in_specs.append(pl.BlockSpec(memory_space=pl.ANY))
                aliases[len(args)] = 1 + c
                args.append(caches[c])
    kern = functools.partial(_norm_mm_kernel, tn=tn, n_passthrough=len(aliases))
    return pl.pallas_call(
        kern,
        grid=(m // tm, n // tn),
        in_specs=in_specs,
        out_specs=out_specs,
        out_shape=out_shape,
        input_output_aliases=aliases,
        scratch_shapes=[pltpu.VMEM((tm, k), BF16)],
        compiler_params=_params(("parallel", "arbitrary")),
        name="in_proj",
    )(*args)


def _lambda_value(lq, lam_init):
    a = jnp.sum(lq[0:1] * lq[1:2], axis=-1, keepdims=True)
    b = jnp.sum(lq[2:3] * lq[3:4], axis=-1, keepdims=True)
    return jnp.exp(a) - jnp.exp(b) + lam_init


def _online_softmax_update(s, off, vblk, m_ref, l_ref, acc_ref):
    m_old = m_ref[...]
    m_new = jnp.maximum(m_old, jnp.max(s, axis=-1, keepdims=True) + off)
    p = jnp.exp2(s - (m_new - off))
    alpha = jnp.exp2(m_old - m_new)
    l_ref[...] = alpha * l_ref[...] + jnp.sum(p, axis=-1, keepdims=True)
    acc_ref[...] = alpha * acc_ref[...] + jnp.dot(
        p.astype(BF16), vblk, preferred_element_type=F32)
    m_ref[...] = m_new


_NT_DIMS = (((1,), (1,)), ((), ()))


BIAS_SPLIT = 3
ONES_ROWS = 16
HEADS_PER_STEP = 4


def _bf16_pieces(x):
    out = []
    for _ in range(BIAS_SPLIT):
        hi = x.astype(BF16).astype(F32)
        out.append(hi)
        x = x - hi
    return out


def _lane_select(lane, columns):
    out = jnp.zeros_like(columns[0])
    for i, col in enumerate(columns):
        out = jnp.where(lane == i, col, out)
    return out


def _attn_prompt_kernel(slopes_ref, q_ref, k_ref, v_ref, lq_ref, sg_ref, o_ref,
                        kb_ref, vt_ref, qe_ref, mask_ref, m_ref, acc_ref,
                        *, tq, lam_init):
    hp = pl.program_id(1)
    qi = pl.program_id(2)
    n_blk = vt_ref.shape[1]
    seq = kb_ref.shape[1]
    slopes2 = [slopes_ref[hp * HEADS_PER_STEP + g] * LOG2E for g in range(HEADS_PER_STEP)]

    @pl.when(qi == 0)
    def _():
        lane = lax.broadcasted_iota(jnp.int32, (seq, A_DV), 1)
        kidx = (lax.broadcasted_iota(jnp.int32, (seq, A_DV), 0) & (tq - 1)).astype(F32)
        kside = jnp.where(lane < BIAS_SPLIT, kidx, jnp.where(lane < 2 * BIAS_SPLIT, 1.0, 0.0))
        qlane = lax.broadcasted_iota(jnp.int32, (2 * tq, A_DV), 1)
        qidx = (lax.broadcasted_iota(jnp.int32, (2 * tq, A_DV), 0) & (tq - 1)).astype(F32)
        for g in range(HEADS_PER_STEP):
            cs = slice(g * A_DV, (g + 1) * A_DV)
            kb_ref[g, :, :A_DV] = k_ref[:, cs].astype(BF16)
            kb_ref[g, :, A_DV:] = kside.astype(BF16)
            for jb in range(n_blk):
                vt_ref[g, jb, :A_DV, :] = v_ref[jb * tq:(jb + 1) * tq, cs].T.astype(BF16)
                vt_ref[g, jb, A_DV:, :] = jnp.ones((ONES_ROWS, tq), BF16)
            sl = jnp.full((2 * tq, A_DV), slopes2[g], F32)
            cols = _bf16_pieces(sl) + _bf16_pieces(-(sl * qidx))
            qe_ref[g] = _lane_select(qlane, cols).astype(BF16)
        r = lax.broadcasted_iota(jnp.int32, (tq, 2 * tq), 0)
        c = lax.broadcasted_iota(jnp.int32, (tq, 2 * tq), 1) & (tq - 1)
        mask_ref[...] = jnp.where(r <= c, 0.0, NEG_BIG)

    lane = lax.broadcasted_iota(jnp.int32, (tq, A_DV), 1)
    qs = []
    for g in range(HEADS_PER_STEP):
        q = q_ref[:, g * A_DV:(g + 1) * A_DV] * (A_HD ** -0.5 * LOG2E)
        q2 = jnp.concatenate(
            [jnp.where(lane < A_HD, q, 0.0), jnp.where(lane >= A_HD, q, 0.0)], axis=0)
        qs.append(jnp.concatenate([q2.astype(BF16), qe_ref[g]], axis=1))
    m_ref[...] = jnp.full(m_ref.shape, NEG_BIG, F32)
    acc_ref[...] = jnp.zeros(acc_ref.shape, F32)

    def step(g, j, off, mask):
        start = pl.multiple_of(j * tq, tq)
        kblk = kb_ref[g, pl.ds(start, tq), :]
        s = lax.dot_general(kblk, qs[g], _NT_DIMS, preferred_element_type=F32)
        if mask is not None:
            s = s + mask
        m_old = m_ref[g]
        m_new = jnp.maximum(m_old, jnp.max(s, axis=0, keepdims=True) + off)
        p = jnp.exp2(s - (m_new - off))
        alpha = jnp.exp2(m_old - m_new)
        acc_ref[g] = alpha * acc_ref[g] + jnp.dot(
            vt_ref[g, j], p.astype(BF16), preferred_element_type=F32)
        m_ref[g] = m_new

    def body(j, carry):
        rel = ((j - qi) * tq).astype(F32)
        for g in range(HEADS_PER_STEP):
            step(g, j, slopes2[g] * rel, None)
        return carry

    lax.fori_loop(0, qi, body, 0)
    mask = mask_ref[...]
    for g in range(HEADS_PER_STEP):
        step(g, qi, 0.0, mask)

    lam = _lambda_value(lq_ref[...], lam_init)
    for g in range(HEADS_PER_STEP):
        acc = acc_ref[g]
        on = acc[:A_DV] / acc[A_DV:A_DV + 1]
        ot = on[:, :tq] - lam * on[:, tq:]
        yt = ot * lax.rsqrt(jnp.mean(ot * ot, axis=0, keepdims=True) + EPS)
        o_ref[:, g * A_DV:(g + 1) * A_DV] = (
            (yt.T * sg_ref[...]) * (1.0 - lam_init)).astype(BF16)


def attention_prompt(p3, slopes, lam_qk, subln_g3, layer, lam_init, *, tq):
    b, s, _ = p3.shape
    gw = HEADS_PER_STEP * A_DV
    kern = functools.partial(_attn_prompt_kernel, tq=tq, lam_init=lam_init)
    return pl.pallas_call(
        kern,
        grid=(b, A_HEADS // HEADS_PER_STEP, s // tq),
        in_specs=[
            pl.BlockSpec(memory_space=pltpu.SMEM),
            pl.BlockSpec((None, tq, gw), lambda bi, h, qi: (bi, qi, COL_Q // gw + h)),
            pl.BlockSpec((None, s, gw), lambda bi, h, qi: (bi, 0, COL_K // gw + h)),
            pl.BlockSpec((None, s, gw), lambda bi, h, qi: (bi, 0, COL_V // gw + h)),
            pl.BlockSpec((None, 4, A_HD), lambda bi, h, qi: (layer, 0, 0)),
            pl.BlockSpec((None, 1, A_DV), lambda bi, h, qi: (layer, 0, 0)),
        ],
        out_specs=pl.BlockSpec((None, tq, gw), lambda bi, h, qi: (bi, qi, h)),
        out_shape=jax.ShapeDtypeStruct((b, s, BRANCH_W), BF16),
        scratch_shapes=[
            pltpu.VMEM((HEADS_PER_STEP, s, 2 * A_DV), BF16),
            pltpu.VMEM((HEADS_PER_STEP, s // tq, A_DV + ONES_ROWS, tq), BF16),
            pltpu.VMEM((HEADS_PER_STEP, 2 * tq, A_DV), BF16),
            pltpu.VMEM((tq, 2 * tq), F32),
            pltpu.VMEM((HEADS_PER_STEP, 1, 2 * tq), F32),
            pltpu.VMEM((HEADS_PER_STEP, A_DV + ONES_ROWS, 2 * tq), F32),
        ],
        compiler_params=_params(("parallel", "arbitrary", "arbitrary")),
        name="attn_prompt",
    )(slopes, p3, p3, p3, lam_qk, subln_g3)


def _attn_sample_kernel(pt_ref, q_ref, kn_ref, vn_ref, slope_ref, lq_ref, sg_ref, *rest,
                        n_pages, n_steps, past_len, lam_init):
    del pt_ref
    kp_refs = rest[:n_pages]
    vp_refs = rest[n_pages:2 * n_pages]
    o_ref, qall_ref, sb_ref, m_ref, l_ref, acc_ref = rest[2 * n_pages:]
    s_idx = pl.program_id(1)
    n_rows = 2 * A_HEADS * SUBLANES
    half = A_HEADS * SUBLANES
    span = n_pages * PAGE_SIZE
    slope = slope_ref[...] * LOG2E

    def head_cols(ref, h):
        return ref[:, h * A_DV:(h + 1) * A_DV]

    @pl.when(s_idx == 0)
    def _():
        lane = lax.broadcasted_iota(jnp.int32, (SUBLANES, A_DV), 1)
        pieces = []
        for mp in range(2):
            keep = (lane >= A_HD) if mp else (lane < A_HD)
            for h in range(A_HEADS):
                pieces.append(
                    jnp.where(keep, head_cols(q_ref, h) * (A_HD ** -0.5 * LOG2E), 0.0))
        qall_ref[...] = jnp.concatenate(pieces, axis=0).astype(BF16)
        r = lax.broadcasted_iota(jnp.int32, sb_ref.shape, 0)
        c = lax.broadcasted_iota(jnp.int32, sb_ref.shape, 1)
        bias = slope * ((c >> 3) - (r & 7)).astype(F32)
        sb_ref[...] = jnp.where((c & 7) == ((r >> 3) & 7), bias, NEG_BIG)
        m_ref[...] = jnp.full(m_ref.shape, NEG_BIG, F32)
        l_ref[...] = jnp.zeros(l_ref.shape, F32)
        acc_ref[...] = jnp.zeros(acc_ref.shape, F32)

    qall = qall_ref[...]
    kcat = jnp.concatenate([r[...].astype(BF16) for r in kp_refs], axis=0)
    vcat = jnp.concatenate([r[...].astype(BF16) for r in vp_refs], axis=0)
    s = lax.dot_general(qall, kcat, _NT_DIMS, preferred_element_type=F32) + sb_ref[...]
    off = slope * (s_idx * span - past_len).astype(F32)
    _online_softmax_update(s, off, vcat, m_ref, l_ref, acc_ref)

    @pl.when(s_idx == n_steps - 1)
    def _():
        pad = jnp.zeros((PAGE_SIZE - half, A_DV), F32)
        kn = jnp.concatenate([head_cols(kn_ref, h) for h in range(A_HEADS)] + [pad], axis=0)
        vn = jnp.concatenate([head_cols(vn_ref, h) for h in range(A_HEADS)] + [pad], axis=0)
        sn = lax.dot_general(qall, kn.astype(BF16), _NT_DIMS, preferred_element_type=F32)
        r = lax.broadcasted_iota(jnp.int32, sn.shape, 0)
        c = lax.broadcasted_iota(jnp.int32, sn.shape, 1)
        qpos = r & 7
        kpos = c & 7
        valid = (c < half) & ((c >> 3) == ((r >> 3) & 7)) & (kpos <= qpos)
        sn = jnp.where(valid, sn + slope * (kpos - qpos).astype(F32), NEG_BIG)
        _online_softmax_update(sn, 0.0, vn.astype(BF16), m_ref, l_ref, acc_ref)

        o_map = acc_ref[...] / l_ref[...]
        lam = _lambda_value(lq_ref[...], lam_init)
        outs = []
        for h in range(A_HEADS):
            o = (o_map[h * SUBLANES:(h + 1) * SUBLANES]
                 - lam * o_map[half + h * SUBLANES:half + (h + 1) * SUBLANES])
            outs.append(_rms(o, sg_ref[...]) * (1.0 - lam_init))
        o_ref[...] = jnp.concatenate(outs, axis=1).astype(BF16)


def attention_sample(p3, cache_k4, cache_v4, page_table, slope_rows, lam_qk, subln_g3,
                     layer, lam_init, *, n_pages):
    b, t, _ = p3.shape
    assert t == SUBLANES
    pages_total = page_table.shape[1]
    n_steps = pages_total // n_pages
    n_rows = 2 * A_HEADS * SUBLANES
    page_rows = PAGE_SIZE * A_HEADS
    kern = functools.partial(
        _attn_sample_kernel, n_pages=n_pages, n_steps=n_steps,
        past_len=pages_total * PAGE_SIZE, lam_init=lam_init)

    def page_spec(i):
        return pl.BlockSpec(
            (None, None, page_rows, A_DV),
            lambda bi, si, pt: (layer, pt[bi, si * n_pages + i], 0, 0))

    col = lambda c: pl.BlockSpec((None, t, BRANCH_W), lambda bi, si, pt: (bi, 0, c // BRANCH_W))
    grid_spec = pltpu.PrefetchScalarGridSpec(
        num_scalar_prefetch=1,
        grid=(b, n_steps),
        in_specs=[
            col(COL_Q), col(COL_K), col(COL_V),
            pl.BlockSpec((n_rows, 1), lambda bi, si, pt: (0, 0)),
            pl.BlockSpec((None, 4, A_HD), lambda bi, si, pt: (layer, 0, 0)),
            pl.BlockSpec((None, 1, A_DV), lambda bi, si, pt: (layer, 0, 0)),
        ] + [page_spec(i) for i in range(n_pages)] * 2,
        out_specs=pl.BlockSpec((None, t, BRANCH_W), lambda bi, si, pt: (bi, 0, 0)),
        scratch_shapes=[
            pltpu.VMEM((n_rows, A_DV), BF16),
            pltpu.VMEM((n_rows, n_pages * page_rows), F32),
            pltpu.VMEM((n_rows, 1), F32),
            pltpu.VMEM((n_rows, 1), F32),
            pltpu.VMEM((n_rows, A_DV), F32),
        ],
    )
    return pl.pallas_call(
        kern,
        grid_spec=grid_spec,
        out_shape=jax.ShapeDtypeStruct((b, t, BRANCH_W), BF16),
        compiler_params=_params(("parallel", "arbitrary")),
        name="attn_sample",
    )(page_table, p3, p3, p3, slope_rows, lam_qk, subln_g3,
      *([cache_k4] * n_pages), *([cache_v4] * n_pages))


def _shift_rows(x, prev, j, row8):
    xr = pltpu.roll(x, j, 0)
    head = jnp.where(row8 < j, pltpu.roll(prev, j, 0), xr[:SUBLANES])
    if x.shape[0] == SUBLANES:
        return head
    return jnp.concatenate([head, xr[SUBLANES:]], axis=0)


def _lru_kernel(x_ref, c0_ref, h0_ref, cw_ref, cb_ref, wr_ref, wi_ref, br_ref, bi_ref, lam_ref,
                hs_ref, hl_ref, co_ref, hprev_ref, xprev_ref):
    t_idx = pl.program_id(2)

    @pl.when(t_idx == 0)
    def _():
        hprev_ref[...] = h0_ref[...]
        xprev_ref[...] = c0_ref[...]

    x = x_ref[...]
    n_t, tc = x.shape
    prev = xprev_ref[...]
    row8 = lax.broadcasted_iota(jnp.int32, (SUBLANES, tc), 0)
    cw = cw_ref[...]
    xc = cb_ref[...] + cw[0:1] * _shift_rows(x, prev, 3, row8)
    xc = xc + cw[1:2] * _shift_rows(x, prev, 2, row8)
    xc = xc + cw[2:3] * _shift_rows(x, prev, 1, row8)
    xc = xc + cw[3:4] * x
    tail = x[n_t - SUBLANES:]
    xprev_ref[...] = tail
    co_ref[...] = tail

    xcb = xc.astype(BF16)
    r = jax.nn.sigmoid(jnp.dot(xcb, wr_ref[...], preferred_element_type=F32) + br_ref[...])
    i = jax.nn.sigmoid(jnp.dot(xcb, wi_ref[...], preferred_element_type=F32) + bi_ref[...])
    nl = -lam_ref[...]
    softplus = jnp.maximum(nl, 0.0) + jnp.log1p(jnp.exp(-jnp.abs(nl)))
    log_a = (-LRU_C) * r * softplus
    a = jnp.exp(log_a)
    th = jnp.tanh(log_a)
    u = jnp.sqrt(-2.0 * th / (1.0 - th)) * (i * xc)

    n_g = n_t // SUBLANES
    a = a.reshape(n_g, SUBLANES, tc)
    u = u.reshape(n_g, SUBLANES, tc)
    row = lax.broadcasted_iota(jnp.int32, a.shape, 1)
    d = 1
    while d < SUBLANES:
        a_sh = jnp.where(row >= d, pltpu.roll(a, d, 1), 1.0)
        u_sh = jnp.where(row >= d, pltpu.roll(u, d, 1), 0.0)
        u = u + a * u_sh
        a = a * a_sh
        d *= 2
    carry = hprev_ref[...]
    groups = []
    for g in range(n_g):
        groups.append(u[g] + a[g] * carry)
        carry = groups[-1][SUBLANES - 1:]
    hs_ref[...] = jnp.concatenate(groups, axis=0).astype(BF16)
    hprev_ref[...] = carry
    hl_ref[...] = carry


def rg_lru(p3, conv0, h0, conv_w, conv_b3, wr_bd, wi_bd, b_rg3, b_ig3, lam3, layer, *, tt, tc):
    b, s, _ = p3.shape
    c = LRU_W
    vec = lambda: pl.BlockSpec((None, 1, tc), lambda bi, ci, ti: (layer, 0, ci))
    gate_w = lambda: pl.BlockSpec((None, None, tc, tc), lambda bi, ci, ti: (layer, ci, 0, 0))
    state = lambda rows: pl.BlockSpec((None, rows, tc), lambda bi, ci, ti: (bi, 0, ci))
    return pl.pallas_call(
        _lru_kernel,
        grid=(b, c // tc, s // tt),
        in_specs=[
            pl.BlockSpec((None, tt, tc), lambda bi, ci, ti: (bi, ti, COL_XB // tc + ci)),
            state(SUBLANES), state(1),
            pl.BlockSpec((None, CONV_W, tc), lambda bi, ci, ti: (layer, 0, ci)),
            vec(), gate_w(), gate_w(), vec(), vec(), vec(),
        ],
        out_specs=[
            pl.BlockSpec((None, tt, tc), lambda bi, ci, ti: (bi, ti, ci)),
            state(1), state(SUBLANES),
        ],
        out_shape=[
            jax.ShapeDtypeStruct((b, s, c), BF16),
            jax.ShapeDtypeStruct((b, 1, c), F32),
            jax.ShapeDtypeStruct((b, SUBLANES, c), F32),
        ],
        scratch_shapes=[pltpu.VMEM((1, tc), F32), pltpu.VMEM((SUBLANES, tc), F32)],
        compiler_params=_params(("parallel", "parallel", "arbitrary")),
        name="rg_lru",
    )(p3, conv0, h0, conv_w, conv_b3, wr_bd, wi_bd, b_rg3, b_ig3, lam3)


def _gelu_tanh(x):
    c = math.sqrt(2.0 / math.pi)
    return 0.5 * x * (1.0 + jnp.tanh(c * (x + 0.044715 * (x * x * x))))


def _cmlp_kernel(zu_ref, zv_ref, lg_ref, lb_ref, ws_ref, bst_ref, oc_ref, *maybe_vc_ref, ch):
    u = _gelu_tanh(zu_ref[...])
    vg = _gelu_tanh(zv_ref[...])
    xc = vg - jnp.mean(vg, axis=-1, keepdims=True)
    v = xc * lax.rsqrt(jnp.mean(xc * xc, axis=-1, keepdims=True) + EPS) * lg_ref[...] + lb_ref[...]
    if maybe_vc_ref:
        maybe_vc_ref[0][...] = v
    n_ch = u.shape[0] // ch
    r = lax.broadcasted_iota(jnp.int32, (ch, ch), 0)
    c = lax.broadcasted_iota(jnp.int32, (ch, ch), 1)
    for g in range(C_GROUPS):
        wm = jnp.where(c <= r, ws_ref[g][:ch, :ch], 0.0)
        bias = bst_ref[:ch, g:g + 1]
        cs = slice(g * C_GW, (g + 1) * C_GW)
        for n in range(n_ch):
            rs = slice(n * ch, (n + 1) * ch)
            vb = v[rs, cs]
            if ch >= LANES:
                f = jnp.dot(wm.astype(BF16), vb.astype(BF16), preferred_element_type=F32) + bias
            else:
                f = bias + wm[:, 0:1] * vb[0:1, :]
                for s in range(1, ch):
                    f = f + wm[:, s:s + 1] * vb[s:s + 1, :]
            oc_ref[rs, cs] = (u[rs, cs] * f).astype(BF16)


def chunk_mlp(p, ln_g3, ln_b3, w_s, b_st, layer, *, tm, ch, emit_v):
    m = p.shape[0]
    vec = lambda: pl.BlockSpec((None, 1, C_W), lambda i: (layer, 0, 0))
    out_specs = [pl.BlockSpec((tm, C_W), lambda i: (i, 0))]
    out_shape = [jax.ShapeDtypeStruct((m, C_W), BF16)]
    if emit_v:
        out_specs.append(pl.BlockSpec((tm, C_W), lambda i: (i, 0)))
        out_shape.append(jax.ShapeDtypeStruct((m, C_W), F32))
    return pl.pallas_call(
        functools.partial(_cmlp_kernel, ch=ch),
        grid=(m // tm,),
        in_specs=[
            pl.BlockSpec((tm, C_W), lambda i: (i, COL_Z // C_W)),
            pl.BlockSpec((tm, C_W), lambda i: (i, COL_Z // C_W + 1)),
            vec(), vec(),
            pl.BlockSpec((None, C_GROUPS, CHUNK, CHUNK), lambda i: (layer, 0, 0, 0)),
            pl.BlockSpec((None, CHUNK, C_GROUPS), lambda i: (layer, 0, 0)),
        ],
        out_specs=out_specs,
        out_shape=out_shape,
        compiler_params=_params(("parallel",)),
        name="chunk_mlp",
    )(p, p, ln_g3, ln_b3, w_s, b_st)


def _branch_kernel(oa_ref, ob_ref, oc_ref, g0_ref, g1_ref, g2_ref, wb_ref, m_ref):
    acc = None
    for j, (o_ref, g_ref) in enumerate(((oa_ref, g0_ref), (ob_ref, g1_ref), (oc_ref, g2_ref))):
        y = jnp.dot(o_ref[...], wb_ref[j], preferred_element_type=F32)
        t = jax.nn.sigmoid(g_ref[...]) * y
        acc = t if acc is None else acc + t
    m_ref[...] = acc.astype(BF16)


def branch_merge(o_a, o_b, o_c, p, wb, layer, *, tm, tn):
    m = o_a.shape[0]
    o_spec = lambda: pl.BlockSpec((tm, BRANCH_W), lambda i, j: (i, 0))
    g_spec = lambda b: pl.BlockSpec((tm, tn), lambda i, j: (i, (COL_G + b * D_MODEL) // tn + j))
    return pl.pallas_call(
        _branch_kernel,
        grid=(m // tm, D_MODEL // tn),
        in_specs=[o_spec(), o_spec(), o_spec(), g_spec(0), g_spec(1), g_spec(2),
                  pl.BlockSpec((None, N_BRANCH, BRANCH_W, tn), lambda i, j: (layer, 0, 0, j))],
        out_specs=pl.BlockSpec((tm, tn), lambda i, j: (i, j)),
        out_shape=jax.ShapeDtypeStruct((m, D_MODEL), BF16),
        compiler_params=_params(("parallel", "arbitrary")),
        name="branch_merge",
    )(o_a, o_b, o_c, p, p, p, wb)


def _mm_res_kernel(x_ref, w_ref, h_ref, o_ref):
    o_ref[...] = h_ref[...] + jnp.dot(x_ref[...], w_ref[...], preferred_element_type=F32)


def matmul_residual(x, w3, h, layer, *, tm, tn):
    m, k = x.shape
    n = w3.shape[-1]
    return pl.pallas_call(
        _mm_res_kernel,
        grid=(m // tm, n // tn),
        in_specs=[
            pl.BlockSpec((tm, k), lambda i, j: (i, 0)),
            pl.BlockSpec((None, k, tn), lambda i, j: (layer, 0, j)),
            pl.BlockSpec((tm, tn), lambda i, j: (i, j)),
        ],
        out_specs=pl.BlockSpec((tm, tn), lambda i, j: (i, j)),
        out_shape=jax.ShapeDtypeStruct((m, n), F32),
        compiler_params=_params(("parallel", "arbitrary")),
        name="out_proj",
    )(x, w3, h)


def _ffn_kernel(h_ref, g_ref, w1_ref, w2_ref, o_ref, xn_ref):
    @pl.when(pl.program_id(1) == 0)
    def _():
        h = h_ref[...]
        xn_ref[...] = _rms(h, g_ref[...]).astype(BF16)
        o_ref[...] = h

    hid = jnp.maximum(jnp.dot(xn_ref[...], w1_ref[...], preferred_element_type=F32), 0.0)
    o_ref[...] += jnp.dot((hid * hid).astype(BF16), w2_ref[...], preferred_element_type=F32)


def ffn(h, g3, w1, w2, layer, *, tm, tf):
    m, d = h.shape
    return pl.pallas_call(
        _ffn_kernel,
        grid=(m // tm, FF_W // tf),
        in_specs=[
            pl.BlockSpec((tm, d), lambda i, j: (i, 0)),
            pl.BlockSpec((None, 1, d), lambda i, j: (layer, 0, 0)),
            pl.BlockSpec((None, d, tf), lambda i, j: (layer, 0, j)),
            pl.BlockSpec((None, tf, d), lambda i, j: (layer, j, 0)),
        ],
        out_specs=pl.BlockSpec((tm, d), lambda i, j: (i, 0)),
        out_shape=jax.ShapeDtypeStruct((m, d), F32),
        scratch_shapes=[pltpu.VMEM((tm, d), BF16)],
        compiler_params=_params(("parallel", "arbitrary")),
        name="ffn",
    )(h, g3, w1, w2)


def _final_norm_kernel(x_ref, g_ref, o_ref):
    o_ref[...] = _rms(x_ref[...], g_ref[...])


def final_norm(x, g2, *, tm):
    m, d = x.shape
    return pl.pallas_call(
        _final_norm_kernel,
        grid=(m // tm,),
        in_specs=[pl.BlockSpec((tm, d), lambda i: (i, 0)), pl.BlockSpec((1, d), lambda i: (0, 0))],
        out_specs=pl.BlockSpec((tm, d), lambda i: (i, 0)),
        out_shape=jax.ShapeDtypeStruct((m, d), F32),
        compiler_params=_params(("parallel",)),
        name="final_norm",
    )(x, g2)


def _block_diag_gates(w):
    per = 256 // LRU_BW
    w5 = w.reshape(DEPTH, LRU_BLOCKS // per, per, LRU_BW, LRU_BW)
    eye = jnp.eye(per, dtype=w.dtype)
    bd = jnp.einsum("lgbij,bc->lgbicj", w5, eye)
    return bd.reshape(DEPTH, LRU_BLOCKS // per, 256, 256).astype(BF16)


def _trunk_layer(h, layer, wts, attend, conv0, h0, *, tiles, ch, emit_v, kv_stack=None):
    bsz, seq, _ = h.shape
    m = bsz * seq
    h2 = h.reshape(m, D_MODEL)
    if kv_stack is None:
        p = norm_matmul(h2, wts["norm1_g"], wts["w_in"], layer, tm=tiles["tm"], tn=tiles["tn_in"])[0]
    else:
        p, *kv_stack = norm_matmul(
            h2, wts["norm1_g"], wts["w_in"], layer, tm=tiles["tm"], tn=tiles["tn_in"],
            caches=tuple(kv_stack) or None, emit_cache=True)
    p3 = p.reshape(bsz, seq, W_IN)
    o_a = attend(p3).reshape(m, BRANCH_W)
    o_b, h_last, conv_tail = rg_lru(
        p3, conv0, h0, wts["conv_w"], wts["conv_b"], wts["w_rg"], wts["w_ig"],
        wts["b_rg"], wts["b_ig"], wts["lru_lambda"], layer, tt=tiles["tt"], tc=256)
    cm = chunk_mlp(p, wts["cmlp_ln_g"], wts["cmlp_ln_b"], wts["w_s"], wts["b_st"], layer,
                   tm=tiles["tm_c"], ch=ch, emit_v=emit_v)
    mrg = branch_merge(o_a, o_b.reshape(m, BRANCH_W), cm[0], p, wts["w_branch"], layer,
                       tm=tiles["tm"], tn=tiles["tn"])
    h2 = matmul_residual(mrg, wts["w_out"], h2, layer, tm=tiles["tm"], tn=tiles["tn"])
    h2 = ffn(h2, wts["norm2_g"], wts["w_ff1"], wts["w_ff2"], layer, tm=tiles["tm_f"], tf=tiles["tf"])
    if kv_stack is None:
        k = p3[..., COL_K:COL_V].reshape(bsz, seq, A_HEADS, 2 * A_HD)
        v = p3[..., COL_V:COL_XB].reshape(bsz, seq, A_HEADS, A_DV)
    else:
        k, v = kv_stack
    new_buf = conv_tail[:, SUBLANES - (CONV_W - 1):]
    v_c = cm[1].reshape(bsz, seq, C_W) if emit_v else None
    return h2.reshape(bsz, seq, D_MODEL), k, v, new_buf, h_last[:, 0], v_c


def kernel(x_prompt, x_sample, cache_k, cache_v, state_h, state_conv, page_table,
           norm1_g, w_in, lam_qk, subln_g, conv_w, conv_b, w_rg, b_rg, w_ig, b_ig, lru_lambda,
           cmlp_ln_g, cmlp_ln_b, w_s, b_s, w_branch, w_out, norm2_g, w_ff1, w_ff2, final_g):
    bp, sp, _ = x_prompt.shape
    bs, ss, _ = x_sample.shape
    n_pool = cache_k.shape[1]
    row = lambda a: a.reshape(DEPTH, 1, a.shape[-1])
    wts = {
        "norm1_g": row(norm1_g), "w_in": w_in.astype(BF16),
        "conv_w": conv_w, "conv_b": row(conv_b),
        "w_rg": _block_diag_gates(w_rg), "w_ig": _block_diag_gates(w_ig),
        "b_rg": row(b_rg), "b_ig": row(b_ig), "lru_lambda": row(lru_lambda),
        "cmlp_ln_g": row(cmlp_ln_g), "cmlp_ln_b": row(cmlp_ln_b),
        "w_s": w_s, "b_st": jnp.swapaxes(b_s, 1, 2),
        "w_branch": w_branch.astype(BF16), "w_out": w_out.astype(BF16),
        "norm2_g": row(norm2_g), "w_ff1": w_ff1.astype(BF16), "w_ff2": w_ff2.astype(BF16),
    }
    subln_g3 = row(subln_g)
    slopes = jnp.exp2(-8.0 * jnp.arange(1, A_HEADS + 1, dtype=F32) / A_HEADS)
    slope_rows = jnp.tile(jnp.repeat(slopes, SUBLANES), 2).reshape(2 * A_HEADS * SUBLANES, 1)
    cache_k4 = cache_k.reshape(DEPTH, n_pool, PAGE_SIZE * A_HEADS, A_DV)
    cache_v4 = cache_v.reshape(DEPTH, n_pool, PAGE_SIZE * A_HEADS, A_DV)
    conv0_p = jnp.zeros((bp, SUBLANES, LRU_W), F32)
    h0_p = jnp.zeros((bp, 1, LRU_W), F32)
    conv0_s = jnp.pad(state_conv, ((0, 0), (0, 0), (SUBLANES - (CONV_W - 1), 0), (0, 0)))

    tiles_p = dict(tm=1024, tn_in=512, tn=512, tt=256, tm_c=512, tm_f=512, tf=1024)
    ms = bs * ss
    tiles_s = dict(tm=ms, tn_in=512, tn=512, tt=ss, tm_c=ms, tm_f=ms, tf=1024)

    hp, hs = x_prompt, x_sample
    outs = [[] for _ in range(7)]
    kv_stack = ()
    for l in range(DEPTH):
        lam_init = 0.8 - 0.6 * math.exp(-0.3 * l)
        attend_p = functools.partial(
            attention_prompt, slopes=slopes, lam_qk=lam_qk, subln_g3=subln_g3,
            layer=l, lam_init=lam_init, tq=256)
        hp, kp_stack, vp_stack, cp, hlp, _ = _trunk_layer(
            hp, l, wts, attend_p, conv0_p, h0_p, tiles=tiles_p, ch=CHUNK, emit_v=False,
            kv_stack=kv_stack)
        kv_stack = (kp_stack, vp_stack)
        attend_s = functools.partial(
            attention_sample, cache_k4=cache_k4, cache_v4=cache_v4, page_table=page_table,
            slope_rows=slope_rows, lam_qk=lam_qk, subln_g3=subln_g3,
            layer=l, lam_init=lam_init, n_pages=8)
        hs, k_s, v_s, c_s, h_s, vc_s = _trunk_layer(
            hs, l, wts, attend_s, conv0_s[l], state_h[l][:, None, :],
            tiles=tiles_s, ch=min(ss, CHUNK), emit_v=True)
        for lst, val in zip(outs, (hlp, cp, k_s, v_s, h_s, c_s, vc_s)):
            lst.append(val)
    y_prompt = final_norm(hp.reshape(bp * sp, D_MODEL), final_g.reshape(1, D_MODEL), tm=1024)
    y_sample = final_norm(hs.reshape(ms, D_MODEL), final_g.reshape(1, D_MODEL), tm=ms)
    stacked = [jnp.stack(lst) for lst in outs]
    k_prompt, v_prompt = (a.reshape(DEPTH, bp, sp, A_HEADS, A_DV) for a in kv_stack)
    return (y_prompt.reshape(bp, sp, D_MODEL), y_sample.reshape(bs, ss, D_MODEL),
            k_prompt, v_prompt, *stacked)
```

```python
import functools
import math

import jax
import jax.numpy as jnp
from jax import lax
from jax.experimental import pallas as pl
from jax.experimental.pallas import tpu as pltpu

F32 = jnp.float32
BF16 = jnp.bfloat16

D_MODEL = 2048
DEPTH = 4
PAGE_SIZE = 128
BRANCH_W = D_MODEL // 2
A_HD = 64
A_HEADS = BRANCH_W // (2 * A_HD)
A_DV = 2 * A_HD
LRU_W = BRANCH_W
LRU_BLOCKS = 16
LRU_BW = LRU_W // LRU_BLOCKS
CONV_W = 4
LRU_C = 8.0
C_W = BRANCH_W
C_GROUPS = 4
C_GW = C_W // C_GROUPS
CHUNK = 128
N_BRANCH = 3
FF_W = 4 * D_MODEL
EPS = 1e-6

COL_Q = 0
COL_K = BRANCH_W
COL_V = 2 * BRANCH_W
COL_XB = 3 * BRANCH_W
COL_Z = 4 * BRANCH_W
COL_G = 6 * BRANCH_W
W_IN = COL_G + N_BRANCH * D_MODEL

LANES = 128
SUBLANES = 8
NEG_BIG = -1e30
LOG2E = math.log2(math.e)
MIB = 1024 * 1024
VMEM_LIMIT = 52 * MIB


def _params(semantics):
    return pltpu.CompilerParams(dimension_semantics=semantics, vmem_limit_bytes=VMEM_LIMIT)


def _rms(x, g):
    return x * lax.rsqrt(jnp.mean(x * x, axis=-1, keepdims=True) + EPS) * g


def _norm_mm_kernel(x_ref, g_ref, w_ref, *rest, tn, n_passthrough):
    rest = rest[n_passthrough:]
    o_ref, gate_ref, xn_ref = rest[0], rest[1], rest[-1]
    cache_refs = rest[2:-1]
    j = pl.program_id(1)
    n_main = COL_G // tn

    @pl.when(j == 0)
    def _():
        xn_ref[...] = _rms(x_ref[...], g_ref[...]).astype(BF16)

    @pl.when(j < n_main)
    def _():
        o_ref[...] = jnp.dot(xn_ref[...], w_ref[...], preferred_element_type=F32)

    @pl.when(j >= n_main)
    def _():
        pre = jnp.dot(xn_ref[...], w_ref[...], preferred_element_type=F32)
        gate_ref[...] = jax.nn.sigmoid(pre).astype(BF16)

    tm = o_ref.shape[0]
    heads_per_tile = tn // A_DV
    for dst, col0 in zip(cache_refs, (COL_K, COL_V)):
        for t in range(BRANCH_W // tn):
            @pl.when(j == col0 // tn + t)
            def _(dst=dst, t=t):
                for hh in range(heads_per_tile):
                    h = t * heads_per_tile + hh
                    dst[pl.ds(h, tm, stride=A_HEADS), :] = o_ref[:, hh * A_DV:(hh + 1) * A_DV]


def norm_matmul(x, g3, w3, layer, *, tm, tn, caches=None, emit_cache=False):
    m, k = x.shape
    n = w3.shape[-1]
    n_main = COL_G // tn
    in_specs = [
        pl.BlockSpec((tm, k), lambda i, j: (i, 0)),
        pl.BlockSpec((None, 1, k), lambda i, j: (layer, 0, 0)),
        pl.BlockSpec((None, k, tn), lambda i, j: (layer, 0, j)),
    ]
    out_specs = [
        pl.BlockSpec((tm, tn), lambda i, j: (i, jnp.minimum(j, n_main - 1))),
        pl.BlockSpec((tm, tn), lambda i, j: (i, jnp.maximum(j - n_main, 0))),
    ]
    out_shape = [jax.ShapeDtypeStruct((m, COL_G), F32),
                 jax.ShapeDtypeStruct((m, n - COL_G), BF16)]
    args = [x, g3, w3]
    aliases = {}
    if emit_cache:
        for c in range(2):
            out_specs.append(pl.BlockSpec((None, tm * A_HEADS, A_DV), lambda i, j: (layer, i, 0)))
            out_shape.append(jax.ShapeDtypeStruct((DEPTH, m * A_HEADS, A_DV), F32))
            if caches is not None:
                in_specs.append(pl.BlockSpec(memory_space=pl.ANY))
                aliases[len(args)] = 2 + c
                args.append(caches[c])
    kern = functools.partial(_norm_mm_kernel, tn=tn, n_passthrough=len(aliases))
    return pl.pallas_call(
        kern,
        grid=(m // tm, n // tn),
        in_specs=in_specs,
        out_specs=out_specs,
        out_shape=out_shape,
        input_output_aliases=aliases,
        scratch_shapes=[pltpu.VMEM((tm, k), BF16)],
        compiler_params=_params(("parallel", "arbitrary")),
        name="in_proj",
    )(*args)


def _lambda_value(lq, lam_init):
    a = jnp.sum(lq[0:1] * lq[1:2], axis=-1, keepdims=True)
    b = jnp.sum(lq[2:3] * lq[3:4], axis=-1, keepdims=True)
    return jnp.exp(a) - jnp.exp(b) + lam_init


def _online_softmax_update(s, off, vblk, m_ref, l_ref, acc_ref):
    m_old = m_ref[...]
    m_new = jnp.maximum(m_old, jnp.max(s, axis=-1, keepdims=True) + off)
    p = jnp.exp2(s - (m_new - off))
    alpha = jnp.exp2(m_old - m_new)
    l_ref[...] = alpha * l_ref[...] + jnp.sum(p, axis=-1, keepdims=True)
    acc_ref[...] = alpha * acc_ref[...] + jnp.dot(
        p.astype(BF16), vblk, preferred_element_type=F32)
    m_ref[...] = m_new


_NT_DIMS = (((1,), (1,)), ((), ()))


BIAS_SPLIT = 3
ONES_ROWS = 16
HEADS_PER_STEP = 4


def _bf16_pieces(x):
    out = []
    for _ in range(BIAS_SPLIT):
        hi = x.astype(BF16).astype(F32)
        out.append(hi)
        x = x - hi
    return out


def _lane_select(lane, columns):
    out = jnp.zeros_like(columns[0])
    for i, col in enumerate(columns):
        out = jnp.where(lane == i, col, out)
    return out


def _attn_prompt_kernel(slopes_ref, q_ref, k_ref, v_ref, lq_ref, sg_ref, o_ref,
                        kb_ref, vt_ref, qe_ref, mask_ref, m_ref, acc_ref, s_ref, pv_ref, al_ref,
                        *, tq, lam_init):
    hp = pl.program_id(1)
    qi = pl.program_id(2)
    n_blk = vt_ref.shape[1]
    seq = kb_ref.shape[1]
    slopes2 = [slopes_ref[hp * HEADS_PER_STEP + g] * LOG2E for g in range(HEADS_PER_STEP)]

    @pl.when(qi == 0)
    def _():
        lane = lax.broadcasted_iota(jnp.int32, (seq, A_DV), 1)
        kidx = (lax.broadcasted_iota(jnp.int32, (seq, A_DV), 0) & (tq - 1)).astype(F32)
        kside = jnp.where(lane < BIAS_SPLIT, kidx, jnp.where(lane < 2 * BIAS_SPLIT, 1.0, 0.0))
        qlane = lax.broadcasted_iota(jnp.int32, (2 * tq, A_DV), 1)
        qidx = (lax.broadcasted_iota(jnp.int32, (2 * tq, A_DV), 0) & (tq - 1)).astype(F32)
        for g in range(HEADS_PER_STEP):
            cs = slice(g * A_DV, (g + 1) * A_DV)
            kb_ref[g, :, :A_DV] = k_ref[:, cs].astype(BF16)
            kb_ref[g, :, A_DV:] = kside.astype(BF16)
            for jb in range(n_blk):
                vt_ref[g, jb, :A_DV, :] = v_ref[jb * tq:(jb + 1) * tq, cs].T.astype(BF16)
                vt_ref[g, jb, A_DV:, :] = jnp.ones((ONES_ROWS, tq), BF16)
            sl = jnp.full((2 * tq, A_DV), slopes2[g], F32)
            cols = _bf16_pieces(sl) + _bf16_pieces(-(sl * qidx))
            qe_ref[g] = _lane_select(qlane, cols).astype(BF16)
        r = lax.broadcasted_iota(jnp.int32, (tq, 2 * tq), 0)
        c = lax.broadcasted_iota(jnp.int32, (tq, 2 * tq), 1) & (tq - 1)
        mask_ref[...] = jnp.where(r <= c, 0.0, NEG_BIG)

    lane = lax.broadcasted_iota(jnp.int32, (tq, A_DV), 1)
    qs = []
    for g in range(HEADS_PER_STEP):
        q = q_ref[:, g * A_DV:(g + 1) * A_DV] * (A_HD ** -0.5 * LOG2E)
        q2 = jnp.concatenate(
            [jnp.where(lane < A_HD, q, 0.0), jnp.where(lane >= A_HD, q, 0.0)], axis=0)
        qs.append(jnp.concatenate([q2.astype(BF16), qe_ref[g]], axis=1))
    m_ref[...] = jnp.full(m_ref.shape, NEG_BIG, F32)
    acc_ref[...] = jnp.zeros(acc_ref.shape, F32)
    pv_ref[...] = jnp.zeros(pv_ref.shape, F32)
    al_ref[...] = jnp.ones(al_ref.shape, F32)

    def scores(g, j):
        start = pl.multiple_of(j * tq, tq)
        kblk = kb_ref[g, pl.ds(start, tq), :]
        return lax.dot_general(kblk, qs[g], _NT_DIMS, preferred_element_type=F32)

    def softmax_pv(g, j, s, off):
        m_old = m_ref[g]
        m_new = jnp.maximum(m_old, jnp.max(s, axis=0, keepdims=True) + off)
        p = jnp.exp2(s - (m_new - off))
        m_ref[g] = m_new
        pv = jnp.dot(vt_ref[g, j], p.astype(BF16), preferred_element_type=F32)
        return jnp.exp2(m_old - m_new), pv

    for g in range(HEADS_PER_STEP):
        s_ref[0, g] = scores(g, 0)

    def body(j, carry):
        rel = ((j - qi) * tq).astype(F32)
        slot = j & 1
        for g in range(HEADS_PER_STEP):
            s_cur = s_ref[slot, g]
            s_ref[1 - slot, g] = scores(g, j + 1)
            acc_ref[g] = al_ref[g] * acc_ref[g] + pv_ref[g]
            alpha, pv = softmax_pv(g, j, s_cur, slopes2[g] * rel)
            al_ref[g] = alpha
            pv_ref[g] = pv
        return carry

    lax.fori_loop(0, qi, body, 0)
    mask = mask_ref[...]
    lam = _lambda_value(lq_ref[...], lam_init)
    for g in range(HEADS_PER_STEP):
        acc = al_ref[g] * acc_ref[g] + pv_ref[g]
        alpha, pv = softmax_pv(g, qi, s_ref[qi & 1, g] + mask, 0.0)
        acc = alpha * acc + pv
        on = acc[:A_DV] / acc[A_DV:A_DV + 1]
        ot = on[:, :tq] - lam * on[:, tq:]
        yt = ot * lax.rsqrt(jnp.mean(ot * ot, axis=0, keepdims=True) + EPS)
        o_ref[:, g * A_DV:(g + 1) * A_DV] = (
            (yt.T * sg_ref[...]) * (1.0 - lam_init)).astype(BF16)


def attention_prompt(p3, slopes, lam_qk, subln_g3, layer, lam_init, *, tq):
    b, s, _ = p3.shape
    gw = HEADS_PER_STEP * A_DV
    kern = functools.partial(_attn_prompt_kernel, tq=tq, lam_init=lam_init)
    return pl.pallas_call(
        kern,
        grid=(b, A_HEADS // HEADS_PER_STEP, s // tq),
        in_specs=[
            pl.BlockSpec(memory_space=pltpu.SMEM),
            pl.BlockSpec((None, tq, gw), lambda bi, h, qi: (bi, qi, COL_Q // gw + h)),
            pl.BlockSpec((None, s, gw), lambda bi, h, qi: (bi, 0, COL_K // gw + h)),
            pl.BlockSpec((None, s, gw), lambda bi, h, qi: (bi, 0, COL_V // gw + h)),
            pl.BlockSpec((None, 4, A_HD), lambda bi, h, qi: (layer, 0, 0)),
            pl.BlockSpec((None, 1, A_DV), lambda bi, h, qi: (layer, 0, 0)),
        ],
        out_specs=pl.BlockSpec((None, tq, gw), lambda bi, h, qi: (bi, qi, h)),
        out_shape=jax.ShapeDtypeStruct((b, s, BRANCH_W), BF16),
        scratch_shapes=[
            pltpu.VMEM((HEADS_PER_STEP, s, 2 * A_DV), BF16),
            pltpu.VMEM((HEADS_PER_STEP, s // tq, A_DV + ONES_ROWS, tq), BF16),
            pltpu.VMEM((HEADS_PER_STEP, 2 * tq, A_DV), BF16),
            pltpu.VMEM((tq, 2 * tq), F32),
            pltpu.VMEM((HEADS_PER_STEP, 1, 2 * tq), F32),
            pltpu.VMEM((HEADS_PER_STEP, A_DV + ONES_ROWS, 2 * tq), F32),
            pltpu.VMEM((2, HEADS_PER_STEP, tq, 2 * tq), F32),
            pltpu.VMEM((HEADS_PER_STEP, A_DV + ONES_ROWS, 2 * tq), F32),
            pltpu.VMEM((HEADS_PER_STEP, 1, 2 * tq), F32),
        ],
        compiler_params=_params(("parallel", "arbitrary", "arbitrary")),
        name="attn_prompt",
    )(slopes, p3, p3, p3, lam_qk, subln_g3)


def _attn_sample_kernel(pt_ref, q_ref, kn_ref, vn_ref, slope_ref, lq_ref, sg_ref, *rest,
                        n_pages, n_steps, past_len, lam_init):
    del pt_ref
    kp_refs = rest[:n_pages]
    vp_refs = rest[n_pages:2 * n_pages]
    o_ref, qall_ref, sb_ref, m_ref, l_ref, acc_ref = rest[2 * n_pages:]
    s_idx = pl.program_id(1)
    n_rows = 2 * A_HEADS * SUBLANES
    half = A_HEADS * SUBLANES
    span = n_pages * PAGE_SIZE
    slope = slope_ref[...] * LOG2E

    def head_cols(ref, h):
        return ref[:, h * A_DV:(h + 1) * A_DV]

    @pl.when(s_idx == 0)
    def _():
        lane = lax.broadcasted_iota(jnp.int32, (SUBLANES, A_DV), 1)
        pieces = []
        for mp in range(2):
            keep = (lane >= A_HD) if mp else (lane < A_HD)
            for h in range(A_HEADS):
                pieces.append(
                    jnp.where(keep, head_cols(q_ref, h) * (A_HD ** -0.5 * LOG2E), 0.0))
        qall_ref[...] = jnp.concatenate(pieces, axis=0).astype(BF16)
        r = lax.broadcasted_iota(jnp.int32, sb_ref.shape, 0)
        c = lax.broadcasted_iota(jnp.int32, sb_ref.shape, 1)
        bias = slope * ((c >> 3) - (r & 7)).astype(F32)
        sb_ref[...] = jnp.where((c & 7) == ((r >> 3) & 7), bias, NEG_BIG)
        m_ref[...] = jnp.full(m_ref.shape, NEG_BIG, F32)
        l_ref[...] = jnp.zeros(l_ref.shape, F32)
        acc_ref[...] = jnp.zeros(acc_ref.shape, F32)

    qall = qall_ref[...]
    kcat = jnp.concatenate([r[...].astype(BF16) for r in kp_refs], axis=0)
    vcat = jnp.concatenate([r[...].astype(BF16) for r in vp_refs], axis=0)
    s = lax.dot_general(qall, kcat, _NT_DIMS, preferred_element_type=F32) + sb_ref[...]
    off = slope * (s_idx * span - past_len).astype(F32)
    _online_softmax_update(s, off, vcat, m_ref, l_ref, acc_ref)

    @pl.when(s_idx == n_steps - 1)
    def _():
        pad = jnp.zeros((PAGE_SIZE - half, A_DV), F32)
        kn = jnp.concatenate([head_cols(kn_ref, h) for h in range(A_HEADS)] + [pad], axis=0)
        vn = jnp.concatenate([head_cols(vn_ref, h) for h in range(A_HEADS)] + [pad], axis=0)
        sn = lax.dot_general(qall, kn.astype(BF16), _NT_DIMS, preferred_element_type=F32)
        r = lax.broadcasted_iota(jnp.int32, sn.shape, 0)
        c = lax.broadcasted_iota(jnp.int32, sn.shape, 1)
        qpos = r & 7
        kpos = c & 7
        valid = (c < half) & ((c >> 3) == ((r >> 3) & 7)) & (kpos <= qpos)
        sn = jnp.where(valid, sn + slope * (kpos - qpos).astype(F32), NEG_BIG)
        _online_softmax_update(sn, 0.0, vn.astype(BF16), m_ref, l_ref, acc_ref)

        o_map = acc_ref[...] / l_ref[...]
        lam = _lambda_value(lq_ref[...], lam_init)
        outs = []
        for h in range(A_HEADS):
            o = (o_map[h * SUBLANES:(h + 1) * SUBLANES]
                 - lam * o_map[half + h * SUBLANES:half + (h + 1) * SUBLANES])
            outs.append(_rms(o, sg_ref[...]) * (1.0 - lam_init))
        o_ref[...] = jnp.concatenate(outs, axis=1).astype(BF16)


def attention_sample(p3, cache_k4, cache_v4, page_table, slope_rows, lam_qk, subln_g3,
                     layer, lam_init, *, n_pages):
    b, t, _ = p3.shape
    assert t == SUBLANES
    pages_total = page_table.shape[1]
    n_steps = pages_total // n_pages
    n_rows = 2 * A_HEADS * SUBLANES
    page_rows = PAGE_SIZE * A_HEADS
    kern = functools.partial(
        _attn_sample_kernel, n_pages=n_pages, n_steps=n_steps,
        past_len=pages_total * PAGE_SIZE, lam_init=lam_init)

    def page_spec(i):
        return pl.BlockSpec(
            (None, None, page_rows, A_DV),
            lambda bi, si, pt: (layer, pt[bi, si * n_pages + i], 0, 0))

    col = lambda c: pl.BlockSpec((None, t, BRANCH_W), lambda bi, si, pt: (bi, 0, c // BRANCH_W))
    grid_spec = pltpu.PrefetchScalarGridSpec(
        num_scalar_prefetch=1,
        grid=(b, n_steps),
        in_specs=[
            col(COL_Q), col(COL_K), col(COL_V),
            pl.BlockSpec((n_rows, 1), lambda bi, si, pt: (0, 0)),
            pl.BlockSpec((None, 4, A_HD), lambda bi, si, pt: (layer, 0, 0)),
            pl.BlockSpec((None, 1, A_DV), lambda bi, si, pt: (layer, 0, 0)),
        ] + [page_spec(i) for i in range(n_pages)] * 2,
        out_specs=pl.BlockSpec((None, t, BRANCH_W), lambda bi, si, pt: (bi, 0, 0)),
        scratch_shapes=[
            pltpu.VMEM((n_rows, A_DV), BF16),
            pltpu.VMEM((n_rows, n_pages * page_rows), F32),
            pltpu.VMEM((n_rows, 1), F32),
            pltpu.VMEM((n_rows, 1), F32),
            pltpu.VMEM((n_rows, A_DV), F32),
        ],
    )
    return pl.pallas_call(
        kern,
        grid_spec=grid_spec,
        out_shape=jax.ShapeDtypeStruct((b, t, BRANCH_W), BF16),
        compiler_params=_params(("parallel", "arbitrary")),
        name="attn_sample",
    )(page_table, p3, p3, p3, slope_rows, lam_qk, subln_g3,
      *([cache_k4] * n_pages), *([cache_v4] * n_pages))


def _shift_rows(x, prev, j, row8):
    xr = pltpu.roll(x, j, 0)
    head = jnp.where(row8 < j, pltpu.roll(prev, j, 0), xr[:SUBLANES])
    if x.shape[0] == SUBLANES:
        return head
    return jnp.concatenate([head, xr[SUBLANES:]], axis=0)


def _lru_kernel(x_ref, c0_ref, h0_ref, cw_ref, cb_ref, wr_ref, wi_ref, br_ref, bi_ref, lam_ref,
                hs_ref, hl_ref, co_ref, hprev_ref, xprev_ref):
    t_idx = pl.program_id(2)

    @pl.when(t_idx == 0)
    def _():
        hprev_ref[...] = h0_ref[...]
        xprev_ref[...] = c0_ref[...]

    x = x_ref[...]
    n_t, tc = x.shape
    prev = xprev_ref[...]
    row8 = lax.broadcasted_iota(jnp.int32, (SUBLANES, tc), 0)
    cw = cw_ref[...]
    xc = cb_ref[...] + cw[0:1] * _shift_rows(x, prev, 3, row8)
    xc = xc + cw[1:2] * _shift_rows(x, prev, 2, row8)
    xc = xc + cw[2:3] * _shift_rows(x, prev, 1, row8)
    xc = xc + cw[3:4] * x
    tail = x[n_t - SUBLANES:]
    xprev_ref[...] = tail
    co_ref[...] = tail

    xcb = xc.astype(BF16)
    r = jax.nn.sigmoid(jnp.dot(xcb, wr_ref[...], preferred_element_type=F32) + br_ref[...])
    i = jax.nn.sigmoid(jnp.dot(xcb, wi_ref[...], preferred_element_type=F32) + bi_ref[...])
    nl = -lam_ref[...]
    softplus = jnp.maximum(nl, 0.0) + jnp.log1p(jnp.exp(-jnp.abs(nl)))
    log_a = (-LRU_C) * r * softplus
    a = jnp.exp(log_a)
    th = jnp.tanh(log_a)
    u = jnp.sqrt(-2.0 * th / (1.0 - th)) * (i * xc)

    n_g = n_t // SUBLANES
    a = a.reshape(n_g, SUBLANES, tc)
    u = u.reshape(n_g, SUBLANES, tc)
    row = lax.broadcasted_iota(jnp.int32, a.shape, 1)
    d = 1
    while d < SUBLANES:
        a_sh = jnp.where(row >= d, pltpu.roll(a, d, 1), 1.0)
        u_sh = jnp.where(row >= d, pltpu.roll(u, d, 1), 0.0)
        u = u + a * u_sh
        a = a * a_sh
        d *= 2
    carry = hprev_ref[...]
    groups = []
    for g in range(n_g):
        groups.append(u[g] + a[g] * carry)
        carry = groups[-1][SUBLANES - 1:]
    hs_ref[...] = jnp.concatenate(groups, axis=0).astype(BF16)
    hprev_ref[...] = carry
    hl_ref[...] = carry


def rg_lru(p3, conv0, h0, conv_w, conv_b3, wr_bd, wi_bd, b_rg3, b_ig3, lam3, layer, *, tt, tc):
    b, s, _ = p3.shape
    c = LRU_W
    vec = lambda: pl.BlockSpec((None, 1, tc), lambda bi, ci, ti: (layer, 0, ci))
    gate_w = lambda: pl.BlockSpec((None, None, tc, tc), lambda bi, ci, ti: (layer, ci, 0, 0))
    state = lambda rows: pl.BlockSpec((None, rows, tc), lambda bi, ci, ti: (bi, 0, ci))
    return pl.pallas_call(
        _lru_kernel,
        grid=(b, c // tc, s // tt),
        in_specs=[
            pl.BlockSpec((None, tt, tc), lambda bi, ci, ti: (bi, ti, COL_XB // tc + ci)),
            state(SUBLANES), state(1),
            pl.BlockSpec((None, CONV_W, tc), lambda bi, ci, ti: (layer, 0, ci)),
            vec(), gate_w(), gate_w(), vec(), vec(), vec(),
        ],
        out_specs=[
            pl.BlockSpec((None, tt, tc), lambda bi, ci, ti: (bi, ti, ci)),
            state(1), state(SUBLANES),
        ],
        out_shape=[
            jax.ShapeDtypeStruct((b, s, c), BF16),
            jax.ShapeDtypeStruct((b, 1, c), F32),
            jax.ShapeDtypeStruct((b, SUBLANES, c), F32),
        ],
        scratch_shapes=[pltpu.VMEM((1, tc), F32), pltpu.VMEM((SUBLANES, tc), F32)],
        compiler_params=_params(("parallel", "parallel", "arbitrary")),
        name="rg_lru",
    )(p3, conv0, h0, conv_w, conv_b3, wr_bd, wi_bd, b_rg3, b_ig3, lam3)


def _gelu_tanh(x):
    c = math.sqrt(2.0 / math.pi)
    return 0.5 * x * (1.0 + jnp.tanh(c * (x + 0.044715 * (x * x * x))))


def _cmlp_kernel(zu_ref, zv_ref, lg_ref, lb_ref, ws_ref, bst_ref, oc_ref, *maybe_vc_ref, ch):
    u = _gelu_tanh(zu_ref[...])
    vg = _gelu_tanh(zv_ref[...])
    xc = vg - jnp.mean(vg, axis=-1, keepdims=True)
    v = xc * lax.rsqrt(jnp.mean(xc * xc, axis=-1, keepdims=True) + EPS) * lg_ref[...] + lb_ref[...]
    if maybe_vc_ref:
        maybe_vc_ref[0][...] = v
    n_ch = u.shape[0] // ch
    r = lax.broadcasted_iota(jnp.int32, (ch, ch), 0)
    c = lax.broadcasted_iota(jnp.int32, (ch, ch), 1)
    for g in range(C_GROUPS):
        wm = jnp.where(c <= r, ws_ref[g][:ch, :ch], 0.0)
        bias = bst_ref[:ch, g:g + 1]
        cs = slice(g * C_GW, (g + 1) * C_GW)
        for n in range(n_ch):
            rs = slice(n * ch, (n + 1) * ch)
            vb = v[rs, cs]
            if ch >= LANES:
                f = jnp.dot(wm.astype(BF16), vb.astype(BF16), preferred_element_type=F32) + bias
            else:
                f = bias + wm[:, 0:1] * vb[0:1, :]
                for s in range(1, ch):
                    f = f + wm[:, s:s + 1] * vb[s:s + 1, :]
            oc_ref[rs, cs] = (u[rs, cs] * f).astype(BF16)


def chunk_mlp(p, ln_g3, ln_b3, w_s, b_st, layer, *, tm, ch, emit_v):
    m = p.shape[0]
    vec = lambda: pl.BlockSpec((None, 1, C_W), lambda i: (layer, 0, 0))
    out_specs = [pl.BlockSpec((tm, C_W), lambda i: (i, 0))]
    out_shape = [jax.ShapeDtypeStruct((m, C_W), BF16)]
    if emit_v:
        out_specs.append(pl.BlockSpec((tm, C_W), lambda i: (i, 0)))
        out_shape.append(jax.ShapeDtypeStruct((m, C_W), F32))
    return pl.pallas_call(
        functools.partial(_cmlp_kernel, ch=ch),
        grid=(m // tm,),
        in_specs=[
            pl.BlockSpec((tm, C_W), lambda i: (i, COL_Z // C_W)),
            pl.BlockSpec((tm, C_W), lambda i: (i, COL_Z // C_W + 1)),
            vec(), vec(),
            pl.BlockSpec((None, C_GROUPS, CHUNK, CHUNK), lambda i: (layer, 0, 0, 0)),
            pl.BlockSpec((None, CHUNK, C_GROUPS), lambda i: (layer, 0, 0)),
        ],
        out_specs=out_specs,
        out_shape=out_shape,
        compiler_params=_params(("parallel",)),
        name="chunk_mlp",
    )(p, p, ln_g3, ln_b3, w_s, b_st)


def _branch_kernel(oa_ref, ob_ref, oc_ref, g0_ref, g1_ref, g2_ref, wb_ref, m_ref):
    acc = None
    for j, (o_ref, g_ref) in enumerate(((oa_ref, g0_ref), (ob_ref, g1_ref), (oc_ref, g2_ref))):
        y = jnp.dot(o_ref[...], wb_ref[j], preferred_element_type=F32)
        t = g_ref[...].astype(F32) * y
        acc = t if acc is None else acc + t
    m_ref[...] = acc.astype(BF16)


def branch_merge(o_a, o_b, o_c, gates, wb, layer, *, tm, tn):
    m = o_a.shape[0]
    o_spec = lambda: pl.BlockSpec((tm, BRANCH_W), lambda i, j: (i, 0))
    g_spec = lambda b: pl.BlockSpec((tm, tn), lambda i, j: (i, b * D_MODEL // tn + j))
    return pl.pallas_call(
        _branch_kernel,
        grid=(m // tm, D_MODEL // tn),
        in_specs=[o_spec(), o_spec(), o_spec(), g_spec(0), g_spec(1), g_spec(2),
                  pl.BlockSpec((None, N_BRANCH, BRANCH_W, tn), lambda i, j: (layer, 0, 0, j))],
        out_specs=pl.BlockSpec((tm, tn), lambda i, j: (i, j)),
        out_shape=jax.ShapeDtypeStruct((m, D_MODEL), BF16),
        compiler_params=_params(("parallel", "arbitrary")),
        name="branch_merge",
    )(o_a, o_b, o_c, gates, gates, gates, wb)


def _mm_res_kernel(x_ref, w_ref, h_ref, o_ref):
    o_ref[...] = h_ref[...] + jnp.dot(x_ref[...], w_ref[...], preferred_element_type=F32)


def matmul_residual(x, w3, h, layer, *, tm, tn):
    m, k = x.shape
    n = w3.shape[-1]
    return pl.pallas_call(
        _mm_res_kernel,
        grid=(m // tm, n // tn),
        in_specs=[
            pl.BlockSpec((tm, k), lambda i, j: (i, 0)),
            pl.BlockSpec((None, k, tn), lambda i, j: (layer, 0, j)),
            pl.BlockSpec((tm, tn), lambda i, j: (i, j)),
        ],
        out_specs=pl.BlockSpec((tm, tn), lambda i, j: (i, j)),
        out_shape=jax.ShapeDtypeStruct((m, n), F32),
        compiler_params=_params(("parallel", "arbitrary")),
        name="out_proj",
    )(x, w3, h)


def _ffn_kernel(h_ref, g_ref, w1_ref, w2_ref, o_ref, xn_ref):
    @pl.when(pl.program_id(1) == 0)
    def _():
        h = h_ref[...]
        xn_ref[...] = _rms(h, g_ref[...]).astype(BF16)
        o_ref[...] = h

    hid = jnp.maximum(jnp.dot(xn_ref[...], w1_ref[...], preferred_element_type=F32), 0.0)
    o_ref[...] += jnp.dot((hid * hid).astype(BF16), w2_ref[...], preferred_element_type=F32)


def ffn(h, g3, w1, w2, layer, *, tm, tf):
    m, d = h.shape
    return pl.pallas_call(
        _ffn_kernel,
        grid=(m // tm, FF_W // tf),
        in_specs=[
            pl.BlockSpec((tm, d), lambda i, j: (i, 0)),
            pl.BlockSpec((None, 1, d), lambda i, j: (layer, 0, 0)),
            pl.BlockSpec((None, d, tf), lambda i, j: (layer, 0, j)),
            pl.BlockSpec((None, tf, d), lambda i, j: (layer, j, 0)),
        ],
        out_specs=pl.BlockSpec((tm, d), lambda i, j: (i, 0)),
        out_shape=jax.ShapeDtypeStruct((m, d), F32),
        scratch_shapes=[pltpu.VMEM((tm, d), BF16)],
        compiler_params=_params(("parallel", "arbitrary")),
        name="ffn",
    )(h, g3, w1, w2)


def _final_norm_kernel(x_ref, g_ref, o_ref):
    o_ref[...] = _rms(x_ref[...], g_ref[...])


def final_norm(x, g2, *, tm):
    m, d = x.shape
    return pl.pallas_call(
        _final_norm_kernel,
        grid=(m // tm,),
        in_specs=[pl.BlockSpec((tm, d), lambda i: (i, 0)), pl.BlockSpec((1, d), lambda i: (0, 0))],
        out_specs=pl.BlockSpec((tm, d), lambda i: (i, 0)),
        out_shape=jax.ShapeDtypeStruct((m, d), F32),
        compiler_params=_params(("parallel",)),
        name="final_norm",
    )(x, g2)


def _block_diag_gates(w):
    per = 256 // LRU_BW
    w5 = w.reshape(DEPTH, LRU_BLOCKS // per, per, LRU_BW, LRU_BW)
    eye = jnp.eye(per, dtype=w.dtype)
    bd = jnp.einsum("lgbij,bc->lgbicj", w5, eye)
    return bd.reshape(DEPTH, LRU_BLOCKS // per, 256, 256).astype(BF16)


def _trunk_layer(h, layer, wts, attend, conv0, h0, *, tiles, ch, emit_v, kv_stack=None):
    bsz, seq, _ = h.shape
    m = bsz * seq
    h2 = h.reshape(m, D_MODEL)
    if kv_stack is None:
        p, gates = norm_matmul(
            h2, wts["norm1_g"], wts["w_in"], layer, tm=tiles["tm"], tn=tiles["tn_in"])
    else:
        p, gates, *kv_stack = norm_matmul(
            h2, wts["norm1_g"], wts["w_in"], layer, tm=tiles["tm"], tn=tiles["tn_in"],
            caches=tuple(kv_stack) or None, emit_cache=True)
    p3 = p.reshape(bsz, seq, COL_G)
    o_a = attend(p3).reshape(m, BRANCH_W)
    o_b, h_last, conv_tail = rg_lru(
        p3, conv0, h0, wts["conv_w"], wts["conv_b"], wts["w_rg"], wts["w_ig"],
        wts["b_rg"], wts["b_ig"], wts["lru_lambda"], layer, tt=tiles["tt"], tc=256)
    cm = chunk_mlp(p, wts["cmlp_ln_g"], wts["cmlp_ln_b"], wts["w_s"], wts["b_st"], layer,
                   tm=tiles["tm_c"], ch=ch, emit_v=emit_v)
    mrg = branch_merge(o_a, o_b.reshape(m, BRANCH_W), cm[0], gates, wts["w_branch"], layer,
                       tm=tiles["tm"], tn=tiles["tn"])
    h2 = matmul_residual(mrg, wts["w_out"], h2, layer, tm=tiles["tm"], tn=tiles["tn"])
    h2 = ffn(h2, wts["norm2_g"], wts["w_ff1"], wts["w_ff2"], layer, tm=tiles["tm_f"], tf=tiles["tf"])
    if kv_stack is None:
        k = p3[..., COL_K:COL_V].reshape(bsz, seq, A_HEADS, 2 * A_HD)
        v = p3[..., COL_V:COL_XB].reshape(bsz, seq, A_HEADS, A_DV)
    else:
        k, v = kv_stack
    new_buf = conv_tail[:, SUBLANES - (CONV_W - 1):]
    v_c = cm[1].reshape(bsz, seq, C_W) if emit_v else None
    return h2.reshape(bsz, seq, D_MODEL), k, v, new_buf, h_last[:, 0], v_c


def kernel(x_prompt, x_sample, cache_k, cache_v, state_h, state_conv, page_table,
           norm1_g, w_in, lam_qk, subln_g, conv_w, conv_b, w_rg, b_rg, w_ig, b_ig, lru_lambda,
           cmlp_ln_g, cmlp_ln_b, w_s, b_s, w_branch, w_out, norm2_g, w_ff1, w_ff2, final_g):
    bp, sp, _ = x_prompt.shape
    bs, ss, _ = x_sample.shape
    n_pool = cache_k.shape[1]
    row = lambda a: a.reshape(DEPTH, 1, a.shape[-1])
    wts = {
        "norm1_g": row(norm1_g), "w_in": w_in.astype(BF16),
        "conv_w": conv_w, "conv_b": row(conv_b),
        "w_rg": _block_diag_gates(w_rg), "w_ig": _block_diag_gates(w_ig),
        "b_rg": row(b_rg), "b_ig": row(b_ig), "lru_lambda": row(lru_lambda),
        "cmlp_ln_g": row(cmlp_ln_g), "cmlp_ln_b": row(cmlp_ln_b),
        "w_s": w_s, "b_st": jnp.swapaxes(b_s, 1, 2),
        "w_branch": w_branch.astype(BF16), "w_out": w_out.astype(BF16),
        "norm2_g": row(norm2_g), "w_ff1": w_ff1.astype(BF16), "w_ff2": w_ff2.astype(BF16),
    }
    subln_g3 = row(subln_g)
    slopes = jnp.exp2(-8.0 * jnp.arange(1, A_HEADS + 1, dtype=F32) / A_HEADS)
    slope_rows = jnp.tile(jnp.repeat(slopes, SUBLANES), 2).reshape(2 * A_HEADS * SUBLANES, 1)
    cache_k4 = cache_k.reshape(DEPTH, n_pool, PAGE_SIZE * A_HEADS, A_DV)
    cache_v4 = cache_v.reshape(DEPTH, n_pool, PAGE_SIZE * A_HEADS, A_DV)
    conv0_p = jnp.zeros((bp, SUBLANES, LRU_W), F32)
    h0_p = jnp.zeros((bp, 1, LRU_W), F32)
    conv0_s = jnp.pad(state_conv, ((0, 0), (0, 0), (SUBLANES - (CONV_W - 1), 0), (0, 0)))

    tiles_p = dict(tm=1024, tn_in=512, tn=512, tt=256, tm_c=512, tm_f=512, tf=1024)
    ms = bs * ss
    tiles_s = dict(tm=ms, tn_in=512, tn=512, tt=ss, tm_c=ms, tm_f=ms, tf=1024)

    hp, hs = x_prompt, x_sample
    outs = [[] for _ in range(7)]
    kv_stack = ()
    for l in range(DEPTH):
        lam_init = 0.8 - 0.6 * math.exp(-0.3 * l)
        attend_p = functools.partial(
            attention_prompt, slopes=slopes, lam_qk=lam_qk, subln_g3=subln_g3,
            layer=l, lam_init=lam_init, tq=256)
        hp, kp_stack, vp_stack, cp, hlp, _ = _trunk_layer(
            hp, l, wts, attend_p, conv0_p, h0_p, tiles=tiles_p, ch=CHUNK, emit_v=False,
            kv_stack=kv_stack)
        kv_stack = (kp_stack, vp_stack)
        attend_s = functools.partial(
            attention_sample, cache_k4=cache_k4, cache_v4=cache_v4, page_table=page_table,
            slope_rows=slope_rows, lam_qk=lam_qk, subln_g3=subln_g3,
            layer=l, lam_init=lam_init, n_pages=8)
        hs, k_s, v_s, c_s, h_s, vc_s = _trunk_layer(
            hs, l, wts, attend_s, conv0_s[l], state_h[l][:, None, :],
            tiles=tiles_s, ch=min(ss, CHUNK), emit_v=True)
        for lst, val in zip(outs, (hlp, cp, k_s, v_s, h_s, c_s, vc_s)):
            lst.append(val)
    y_prompt = final_norm(hp.reshape(bp * sp, D_MODEL), final_g.reshape(1, D_MODEL), tm=1024)
    y_sample = final_norm(hs.reshape(ms, D_MODEL), final_g.reshape(1, D_MODEL), tm=ms)
    stacked = [jnp.stack(lst) for lst in outs]
    k_prompt, v_prompt = (a.reshape(DEPTH, bp, sp, A_HEADS, A_DV) for a in kv_stack)
    return (y_prompt.reshape(bp, sp, D_MODEL), y_sample.reshape(bs, ss, D_MODEL),
            k_prompt, v_prompt, *stacked)
```

```python
import functools
import math

import jax
import jax.numpy as jnp
from jax import lax
from jax.experimental import pallas as pl
from jax.experimental.pallas import tpu as pltpu

F32 = jnp.float32
BF16 = jnp.bfloat16

D_MODEL = 2048
DEPTH = 4
PAGE_SIZE = 128
BRANCH_W = D_MODEL // 2
A_HD = 64
A_HEADS = BRANCH_W // (2 * A_HD)
A_DV = 2 * A_HD
LRU_W = BRANCH_W
LRU_BLOCKS = 16
LRU_BW = LRU_W // LRU_BLOCKS
CONV_W = 4
LRU_C = 8.0
C_W = BRANCH_W
C_GROUPS = 4
C_GW = C_W // C_GROUPS
CHUNK = 128
N_BRANCH = 3
FF_W = 4 * D_MODEL
EPS = 1e-6

COL_Q = 0
COL_K = BRANCH_W
COL_V = 2 * BRANCH_W
COL_XB = 3 * BRANCH_W
COL_Z = 4 * BRANCH_W
COL_G = 6 * BRANCH_W
W_IN = COL_G + N_BRANCH * D_MODEL

LANES = 128
SUBLANES = 8
NEG_BIG = -1e30
LOG2E = math.log2(math.e)
MIB = 1024 * 1024
VMEM_LIMIT = 52 * MIB


def _params(semantics):
    return pltpu.CompilerParams(dimension_semantics=semantics, vmem_limit_bytes=VMEM_LIMIT)


def _rms(x, g):
    return x * lax.rsqrt(jnp.mean(x * x, axis=-1, keepdims=True) + EPS) * g


def _norm_mm_kernel(x_ref, g_ref, w_ref, *rest, tn, n_passthrough):
    rest = rest[n_passthrough:]
    o_ref, gate_ref, xn_ref = rest[0], rest[1], rest[-1]
    cache_refs = rest[2:-1]
    j = pl.program_id(1)
    n_main = COL_G // tn

    @pl.when(j == 0)
    def _():
        xn_ref[...] = _rms(x_ref[...], g_ref[...]).astype(BF16)

    @pl.when(j < n_main)
    def _():
        o_ref[...] = jnp.dot(xn_ref[...], w_ref[...], preferred_element_type=F32)

    @pl.when(j >= n_main)
    def _():
        pre = jnp.dot(xn_ref[...], w_ref[...], preferred_element_type=F32)
        gate_ref[...] = (0.5 * (jnp.tanh(0.5 * pre) + 1.0)).astype(BF16)

    tm = o_ref.shape[0]
    heads_per_tile = tn // A_DV
    for dst, col0 in zip(cache_refs, (COL_K, COL_V)):
        for t in range(BRANCH_W // tn):
            @pl.when(j == col0 // tn + t)
            def _(dst=dst, t=t):
                for hh in range(heads_per_tile):
                    h = t * heads_per_tile + hh
                    dst[pl.ds(h, tm, stride=A_HEADS), :] = o_ref[:, hh * A_DV:(hh + 1) * A_DV]


def norm_matmul(x, g3, w3, layer, *, tm, tn, caches=None, emit_cache=False):
    m, k = x.shape
    n = w3.shape[-1]
    n_main = COL_G // tn
    in_specs = [
        pl.BlockSpec((tm, k), lambda i, j: (i, 0), pipeline_mode=pl.Buffered(1)),
        pl.BlockSpec((None, 1, k), lambda i, j: (layer, 0, 0)),
        pl.BlockSpec((None, k, tn), lambda i, j: (layer, 0, j)),
    ]
    out_specs = [
        pl.BlockSpec((tm, tn), lambda i, j: (i, jnp.minimum(j, n_main - 1))),
        pl.BlockSpec((tm, tn), lambda i, j: (i, jnp.maximum(j - n_main, 0))),
    ]
    out_shape = [jax.ShapeDtypeStruct((m, COL_G), F32),
                 jax.ShapeDtypeStruct((m, n - COL_G), BF16)]
    args = [x, g3, w3]
    aliases = {}
    if emit_cache:
        for c in range(2):
            out_specs.append(pl.BlockSpec((None, tm * A_HEADS, A_DV), lambda i, j: (layer, i, 0),
                                          pipeline_mode=pl.Buffered(1)))
            out_shape.append(jax.ShapeDtypeStruct((DEPTH, m * A_HEADS, A_DV), F32))
            if caches is not None:
                in_specs.append(pl.BlockSpec(memory_space=pl.ANY))
                aliases[len(args)] = 2 + c
                args.append(caches[c])
    kern = functools.partial(_norm_mm_kernel, tn=tn, n_passthrough=len(aliases))
    return pl.pallas_call(
        kern,
        grid=(m // tm, n // tn),
        in_specs=in_specs,
        out_specs=out_specs,
        out_shape=out_shape,
        input_output_aliases=aliases,
        scratch_shapes=[pltpu.VMEM((tm, k), BF16)],
        compiler_params=_params(("parallel", "arbitrary")),
        name="in_proj",
    )(*args)


def _lambda_value(lq, lam_init):
    a = jnp.sum(lq[0:1] * lq[1:2], axis=-1, keepdims=True)
    b = jnp.sum(lq[2:3] * lq[3:4], axis=-1, keepdims=True)
    return jnp.exp(a) - jnp.exp(b) + lam_init


def _online_softmax_update(s, off, vblk, m_ref, l_ref, acc_ref):
    m_old = m_ref[...]
    m_new = jnp.maximum(m_old, jnp.max(s, axis=-1, keepdims=True) + off)
    p = jnp.exp2(s - (m_new - off))
    alpha = jnp.exp2(m_old - m_new)
    l_ref[...] = alpha * l_ref[...] + jnp.sum(p, axis=-1, keepdims=True)
    acc_ref[...] = alpha * acc_ref[...] + jnp.dot(
        p.astype(BF16), vblk, preferred_element_type=F32)
    m_ref[...] = m_new


_NT_DIMS = (((1,), (1,)), ((), ()))


BIAS_SPLIT = 3
ONES_ROWS = 16
HEADS_PER_STEP = 4


def _bf16_pieces(x):
    out = []
    for _ in range(BIAS_SPLIT):
        hi = x.astype(BF16).astype(F32)
        out.append(hi)
        x = x - hi
    return out


def _lane_select(lane, columns):
    out = jnp.zeros_like(columns[0])
    for i, col in enumerate(columns):
        out = jnp.where(lane == i, col, out)
    return out


def _attn_prompt_kernel(slopes_ref, q_ref, k_ref, v_ref, lq_ref, sg_ref, o_ref,
                        kb_ref, vt_ref, qe_ref, mask_ref, m_ref, acc_ref, s_ref, pv_ref, al_ref,
                        *, tq, lam_init):
    hp = pl.program_id(1)
    qi = pl.program_id(2)
    n_blk = vt_ref.shape[1]
    seq = kb_ref.shape[1]
    slopes2 = [slopes_ref[hp * HEADS_PER_STEP + g] * LOG2E for g in range(HEADS_PER_STEP)]

    @pl.when(qi == 0)
    def _():
        lane = lax.broadcasted_iota(jnp.int32, (seq, A_DV), 1)
        kidx = (lax.broadcasted_iota(jnp.int32, (seq, A_DV), 0) & (tq - 1)).astype(F32)
        kside = jnp.where(lane < BIAS_SPLIT, kidx, jnp.where(lane < 2 * BIAS_SPLIT, 1.0, 0.0))
        qlane = lax.broadcasted_iota(jnp.int32, (2 * tq, A_DV), 1)
        qidx = (lax.broadcasted_iota(jnp.int32, (2 * tq, A_DV), 0) & (tq - 1)).astype(F32)
        for g in range(HEADS_PER_STEP):
            cs = slice(g * A_DV, (g + 1) * A_DV)
            kb_ref[g, :, :A_DV] = k_ref[:, cs].astype(BF16)
            kb_ref[g, :, A_DV:] = kside.astype(BF16)
            for jb in range(n_blk):
                vt_ref[g, jb, :A_DV, :] = v_ref[jb * tq:(jb + 1) * tq, cs].T.astype(BF16)
                vt_ref[g, jb, A_DV:, :] = jnp.ones((ONES_ROWS, tq), BF16)
            sl = jnp.full((2 * tq, A_DV), slopes2[g], F32)
            cols = _bf16_pieces(sl) + _bf16_pieces(-(sl * qidx))
            qe_ref[g] = _lane_select(qlane, cols).astype(BF16)
        r = lax.broadcasted_iota(jnp.int32, (tq, 2 * tq), 0)
        c = lax.broadcasted_iota(jnp.int32, (tq, 2 * tq), 1) & (tq - 1)
        mask_ref[...] = jnp.where(r <= c, 0.0, NEG_BIG)

    lane = lax.broadcasted_iota(jnp.int32, (tq, A_DV), 1)
    qs = []
    for g in range(HEADS_PER_STEP):
        q = q_ref[:, g * A_DV:(g + 1) * A_DV] * (A_HD ** -0.5 * LOG2E)
        q2 = jnp.concatenate(
            [jnp.where(lane < A_HD, q, 0.0), jnp.where(lane >= A_HD, q, 0.0)], axis=0)
        qs.append(jnp.concatenate([q2.astype(BF16), qe_ref[g]], axis=1))
    m_ref[...] = jnp.full(m_ref.shape, NEG_BIG, F32)
    acc_ref[...] = jnp.zeros(acc_ref.shape, F32)
    pv_ref[...] = jnp.zeros(pv_ref.shape, F32)
    al_ref[...] = jnp.ones(al_ref.shape, F32)

    def scores(g, j):
        start = pl.multiple_of(j * tq, tq)
        kblk = kb_ref[g, pl.ds(start, tq), :]
        return lax.dot_general(kblk, qs[g], _NT_DIMS, preferred_element_type=F32)

    def softmax_pv(g, j, s, off):
        m_old = m_ref[g]
        m_new = jnp.maximum(m_old, jnp.max(s, axis=0, keepdims=True) + off)
        p = jnp.exp2(s - (m_new - off))
        m_ref[g] = m_new
        pv = jnp.dot(vt_ref[g, j], p.astype(BF16), preferred_element_type=F32)
        return jnp.exp2(m_old - m_new), pv

    for g in range(HEADS_PER_STEP):
        s_ref[0, g] = scores(g, 0)

    def body(j, carry):
        rel = ((j - qi) * tq).astype(F32)
        slot = j & 1
        for g in range(HEADS_PER_STEP):
            s_cur = s_ref[slot, g]
            s_ref[1 - slot, g] = scores(g, j + 1)
            acc_ref[g] = al_ref[g] * acc_ref[g] + pv_ref[g]
            alpha, pv = softmax_pv(g, j, s_cur, slopes2[g] * rel)
            al_ref[g] = alpha
            pv_ref[g] = pv
        return carry

    lax.fori_loop(0, qi, body, 0)
    mask = mask_ref[...]
    lam = _lambda_value(lq_ref[...], lam_init)
    for g in range(HEADS_PER_STEP):
        acc = al_ref[g] * acc_ref[g] + pv_ref[g]
        alpha, pv = softmax_pv(g, qi, s_ref[qi & 1, g] + mask, 0.0)
        acc = alpha * acc + pv
        on = acc[:A_DV] / acc[A_DV:A_DV + 1]
        ot = on[:, :tq] - lam * on[:, tq:]
        yt = ot * lax.rsqrt(jnp.mean(ot * ot, axis=0, keepdims=True) + EPS)
        o_ref[:, g * A_DV:(g + 1) * A_DV] = (
            (yt.T * sg_ref[...]) * (1.0 - lam_init)).astype(BF16)


def attention_prompt(p3, slopes, lam_qk, subln_g3, layer, lam_init, *, tq):
    b, s, _ = p3.shape
    gw = HEADS_PER_STEP * A_DV
    kern = functools.partial(_attn_prompt_kernel, tq=tq, lam_init=lam_init)
    return pl.pallas_call(
        kern,
        grid=(b, A_HEADS // HEADS_PER_STEP, s // tq),
        in_specs=[
            pl.BlockSpec(memory_space=pltpu.SMEM),
            pl.BlockSpec((None, tq, gw), lambda bi, h, qi: (bi, qi, COL_Q // gw + h)),
            pl.BlockSpec((None, s, gw), lambda bi, h, qi: (bi, 0, COL_K // gw + h)),
            pl.BlockSpec((None, s, gw), lambda bi, h, qi: (bi, 0, COL_V // gw + h)),
            pl.BlockSpec((None, 4, A_HD), lambda bi, h, qi: (layer, 0, 0)),
            pl.BlockSpec((None, 1, A_DV), lambda bi, h, qi: (layer, 0, 0)),
        ],
        out_specs=pl.BlockSpec((None, tq, gw), lambda bi, h, qi: (bi, qi, h)),
        out_shape=jax.ShapeDtypeStruct((b, s, BRANCH_W), BF16),
        scratch_shapes=[
            pltpu.VMEM((HEADS_PER_STEP, s, 2 * A_DV), BF16),
            pltpu.VMEM((HEADS_PER_STEP, s // tq, A_DV + ONES_ROWS, tq), BF16),
            pltpu.VMEM((HEADS_PER_STEP, 2 * tq, A_DV), BF16),
            pltpu.VMEM((tq, 2 * tq), F32),
            pltpu.VMEM((HEADS_PER_STEP, 1, 2 * tq), F32),
            pltpu.VMEM((HEADS_PER_STEP, A_DV + ONES_ROWS, 2 * tq), F32),
            pltpu.VMEM((2, HEADS_PER_STEP, tq, 2 * tq), F32),
            pltpu.VMEM((HEADS_PER_STEP, A_DV + ONES_ROWS, 2 * tq), F32),
            pltpu.VMEM((HEADS_PER_STEP, 1, 2 * tq), F32),
        ],
        compiler_params=_params(("parallel", "arbitrary", "arbitrary")),
        name="attn_prompt",
    )(slopes, p3, p3, p3, lam_qk, subln_g3)


def _attn_sample_kernel(pt_ref, q_ref, kn_ref, vn_ref, slope_ref, lq_ref, sg_ref, *rest,
                        n_pages, n_steps, past_len, lam_init):
    del pt_ref
    kp_refs = rest[:n_pages]
    vp_refs = rest[n_pages:2 * n_pages]
    o_ref, qall_ref, sb_ref, m_ref, l_ref, acc_ref = rest[2 * n_pages:]
    s_idx = pl.program_id(1)
    n_rows = 2 * A_HEADS * SUBLANES
    half = A_HEADS * SUBLANES
    span = n_pages * PAGE_SIZE
    slope = slope_ref[...] * LOG2E

    def head_cols(ref, h):
        return ref[:, h * A_DV:(h + 1) * A_DV]

    @pl.when(s_idx == 0)
    def _():
        lane = lax.broadcasted_iota(jnp.int32, (SUBLANES, A_DV), 1)
        pieces = []
        for mp in range(2):
            keep = (lane >= A_HD) if mp else (lane < A_HD)
            for h in range(A_HEADS):
                pieces.append(
                    jnp.where(keep, head_cols(q_ref, h) * (A_HD ** -0.5 * LOG2E), 0.0))
        qall_ref[...] = jnp.concatenate(pieces, axis=0).astype(BF16)
        r = lax.broadcasted_iota(jnp.int32, sb_ref.shape, 0)
        c = lax.broadcasted_iota(jnp.int32, sb_ref.shape, 1)
        bias = slope * ((c >> 3) - (r & 7)).astype(F32)
        sb_ref[...] = jnp.where((c & 7) == ((r >> 3) & 7), bias, NEG_BIG)
        m_ref[...] = jnp.full(m_ref.shape, NEG_BIG, F32)
        l_ref[...] = jnp.zeros(l_ref.shape, F32)
        acc_ref[...] = jnp.zeros(acc_ref.shape, F32)

    qall = qall_ref[...]
    kcat = jnp.concatenate([r[...].astype(BF16) for r in kp_refs], axis=0)
    vcat = jnp.concatenate([r[...].astype(BF16) for r in vp_refs], axis=0)
    s = lax.dot_general(qall, kcat, _NT_DIMS, preferred_element_type=F32) + sb_ref[...]
    off = slope * (s_idx * span - past_len).astype(F32)
    _online_softmax_update(s, off, vcat, m_ref, l_ref, acc_ref)

    @pl.when(s_idx == n_steps - 1)
    def _():
        pad = jnp.zeros((PAGE_SIZE - half, A_DV), F32)
        kn = jnp.concatenate([head_cols(kn_ref, h) for h in range(A_HEADS)] + [pad], axis=0)
        vn = jnp.concatenate([head_cols(vn_ref, h) for h in range(A_HEADS)] + [pad], axis=0)
        sn = lax.dot_general(qall, kn.astype(BF16), _NT_DIMS, preferred_element_type=F32)
        r = lax.broadcasted_iota(jnp.int32, sn.shape, 0)
        c = lax.broadcasted_iota(jnp.int32, sn.shape, 1)
        qpos = r & 7
        kpos = c & 7
        valid = (c < half) & ((c >> 3) == ((r >> 3) & 7)) & (kpos <= qpos)
        sn = jnp.where(valid, sn + slope * (kpos - qpos).astype(F32), NEG_BIG)
        _online_softmax_update(sn, 0.0, vn.astype(BF16), m_ref, l_ref, acc_ref)

        o_map = acc_ref[...] / l_ref[...]
        lam = _lambda_value(lq_ref[...], lam_init)
        outs = []
        for h in range(A_HEADS):
            o = (o_map[h * SUBLANES:(h + 1) * SUBLANES]
                 - lam * o_map[half + h * SUBLANES:half + (h + 1) * SUBLANES])
            outs.append(_rms(o, sg_ref[...]) * (1.0 - lam_init))
        o_ref[...] = jnp.concatenate(outs, axis=1).astype(BF16)


def attention_sample(p3, cache_k4, cache_v4, page_table, slope_rows, lam_qk, subln_g3,
                     layer, lam_init, *, n_pages):
    b, t, _ = p3.shape
    assert t == SUBLANES
    pages_total = page_table.shape[1]
    n_steps = pages_total // n_pages
    n_rows = 2 * A_HEADS * SUBLANES
    page_rows = PAGE_SIZE * A_HEADS
    kern = functools.partial(
        _attn_sample_kernel, n_pages=n_pages, n_steps=n_steps,
        past_len=pages_total * PAGE_SIZE, lam_init=lam_init)

    def page_spec(i):
        return pl.BlockSpec(
            (None, None, page_rows, A_DV),
            lambda bi, si, pt: (layer, pt[bi, si * n_pages + i], 0, 0))

    col = lambda c: pl.BlockSpec((None, t, BRANCH_W), lambda bi, si, pt: (bi, 0, c // BRANCH_W))
    grid_spec = pltpu.PrefetchScalarGridSpec(
        num_scalar_prefetch=1,
        grid=(b, n_steps),
        in_specs=[
            col(COL_Q), col(COL_K), col(COL_V),
            pl.BlockSpec((n_rows, 1), lambda bi, si, pt: (0, 0)),
            pl.BlockSpec((None, 4, A_HD), lambda bi, si, pt: (layer, 0, 0)),
            pl.BlockSpec((None, 1, A_DV), lambda bi, si, pt: (layer, 0, 0)),
        ] + [page_spec(i) for i in range(n_pages)] * 2,
        out_specs=pl.BlockSpec((None, t, BRANCH_W), lambda bi, si, pt: (bi, 0, 0)),
        scratch_shapes=[
            pltpu.VMEM((n_rows, A_DV), BF16),
            pltpu.VMEM((n_rows, n_pages * page_rows), F32),
            pltpu.VMEM((n_rows, 1), F32),
            pltpu.VMEM((n_rows, 1), F32),
            pltpu.VMEM((n_rows, A_DV), F32),
        ],
    )
    return pl.pallas_call(
        kern,
        grid_spec=grid_spec,
        out_shape=jax.ShapeDtypeStruct((b, t, BRANCH_W), BF16),
        compiler_params=_params(("parallel", "arbitrary")),
        name="attn_sample",
    )(page_table, p3, p3, p3, slope_rows, lam_qk, subln_g3,
      *([cache_k4] * n_pages), *([cache_v4] * n_pages))


def _shift_rows(x, prev, j, row8):
    xr = pltpu.roll(x, j, 0)
    head = jnp.where(row8 < j, pltpu.roll(prev, j, 0), xr[:SUBLANES])
    if x.shape[0] == SUBLANES:
        return head
    return jnp.concatenate([head, xr[SUBLANES:]], axis=0)


def _lru_kernel(x_ref, c0_ref, h0_ref, cw_ref, cb_ref, wr_ref, wi_ref, br_ref, bi_ref, lam_ref,
                hs_ref, hl_ref, co_ref, hprev_ref, xprev_ref):
    t_idx = pl.program_id(2)

    @pl.when(t_idx == 0)
    def _():
        hprev_ref[...] = h0_ref[...]
        xprev_ref[...] = c0_ref[...]

    x = x_ref[...]
    n_t, tc = x.shape
    prev = xprev_ref[...]
    row8 = lax.broadcasted_iota(jnp.int32, (SUBLANES, tc), 0)
    cw = cw_ref[...]
    xc = cb_ref[...] + cw[0:1] * _shift_rows(x, prev, 3, row8)
    xc = xc + cw[1:2] * _shift_rows(x, prev, 2, row8)
    xc = xc + cw[2:3] * _shift_rows(x, prev, 1, row8)
    xc = xc + cw[3:4] * x
    tail = x[n_t - SUBLANES:]
    xprev_ref[...] = tail
    co_ref[...] = tail

    xcb = xc.astype(BF16)
    r = jax.nn.sigmoid(jnp.dot(xcb, wr_ref[...], preferred_element_type=F32) + br_ref[...])
    i = jax.nn.sigmoid(jnp.dot(xcb, wi_ref[...], preferred_element_type=F32) + bi_ref[...])
    nl = -lam_ref[...]
    softplus = jnp.maximum(nl, 0.0) + jnp.log1p(jnp.exp(-jnp.abs(nl)))
    log_a = (-LRU_C) * r * softplus
    a = jnp.exp(log_a)
    th = jnp.tanh(log_a)
    u = jnp.sqrt(-2.0 * th / (1.0 - th)) * (i * xc)

    n_g = n_t // SUBLANES
    a = a.reshape(n_g, SUBLANES, tc)
    u = u.reshape(n_g, SUBLANES, tc)
    row = lax.broadcasted_iota(jnp.int32, a.shape, 1)
    d = 1
    while d < SUBLANES:
        a_sh = jnp.where(row >= d, pltpu.roll(a, d, 1), 1.0)
        u_sh = jnp.where(row >= d, pltpu.roll(u, d, 1), 0.0)
        u = u + a * u_sh
        a = a * a_sh
        d *= 2
    carry = hprev_ref[...]
    groups = []
    for g in range(n_g):
        groups.append(u[g] + a[g] * carry)
        carry = groups[-1][SUBLANES - 1:]
    hs_ref[...] = jnp.concatenate(groups, axis=0).astype(BF16)
    hprev_ref[...] = carry
    hl_ref[...] = carry


def rg_lru(p3, conv0, h0, conv_w, conv_b3, wr_bd, wi_bd, b_rg3, b_ig3, lam3, layer, *, tt, tc):
    b, s, _ = p3.shape
    c = LRU_W
    vec = lambda: pl.BlockSpec((None, 1, tc), lambda bi, ci, ti: (layer, 0, ci))
    gate_w = lambda: pl.BlockSpec((None, None, tc, tc), lambda bi, ci, ti: (layer, ci, 0, 0))
    state = lambda rows: pl.BlockSpec((None, rows, tc), lambda bi, ci, ti: (bi, 0, ci))
    return pl.pallas_call(
        _lru_kernel,
        grid=(b, c // tc, s // tt),
        in_specs=[
            pl.BlockSpec((None, tt, tc), lambda bi, ci, ti: (bi, ti, COL_XB // tc + ci)),
            state(SUBLANES), state(1),
            pl.BlockSpec((None, CONV_W, tc), lambda bi, ci, ti: (layer, 0, ci)),
            vec(), gate_w(), gate_w(), vec(), vec(), vec(),
        ],
        out_specs=[
            pl.BlockSpec((None, tt, tc), lambda bi, ci, ti: (bi, ti, ci)),
            state(1), state(SUBLANES),
        ],
        out_shape=[
            jax.ShapeDtypeStruct((b, s, c), BF16),
            jax.ShapeDtypeStruct((b, 1, c), F32),
            jax.ShapeDtypeStruct((b, SUBLANES, c), F32),
        ],
        scratch_shapes=[pltpu.VMEM((1, tc), F32), pltpu.VMEM((SUBLANES, tc), F32)],
        compiler_params=_params(("parallel", "parallel", "arbitrary")),
        name="rg_lru",
    )(p3, conv0, h0, conv_w, conv_b3, wr_bd, wi_bd, b_rg3, b_ig3, lam3)


def _gelu_tanh(x):
    c = math.sqrt(2.0 / math.pi)
    return 0.5 * x * (1.0 + jnp.tanh(c * (x + 0.044715 * (x * x * x))))


def _cmlp_kernel(zu_ref, zv_ref, lg_ref, lb_ref, ws_ref, bst_ref, oc_ref, *maybe_vc_ref, ch):
    u = _gelu_tanh(zu_ref[...])
    vg = _gelu_tanh(zv_ref[...])
    xc = vg - jnp.mean(vg, axis=-1, keepdims=True)
    v = xc * lax.rsqrt(jnp.mean(xc * xc, axis=-1, keepdims=True) + EPS) * lg_ref[...] + lb_ref[...]
    if maybe_vc_ref:
        maybe_vc_ref[0][...] = v
    n_ch = u.shape[0] // ch
    r = lax.broadcasted_iota(jnp.int32, (ch, ch), 0)
    c = lax.broadcasted_iota(jnp.int32, (ch, ch), 1)
    for g in range(C_GROUPS):
        wm = jnp.where(c <= r, ws_ref[g][:ch, :ch], 0.0)
        bias = bst_ref[:ch, g:g + 1]
        cs = slice(g * C_GW, (g + 1) * C_GW)
        for n in range(n_ch):
            rs = slice(n * ch, (n + 1) * ch)
            vb = v[rs, cs]
            if ch >= LANES:
                f = jnp.dot(wm.astype(BF16), vb.astype(BF16), preferred_element_type=F32) + bias
            else:
                f = bias + wm[:, 0:1] * vb[0:1, :]
                for s in range(1, ch):
                    f = f + wm[:, s:s + 1] * vb[s:s + 1, :]
            oc_ref[rs, cs] = (u[rs, cs] * f).astype(BF16)


def chunk_mlp(p, ln_g3, ln_b3, w_s, b_st, layer, *, tm, ch, emit_v):
    m = p.shape[0]
    vec = lambda: pl.BlockSpec((None, 1, C_W), lambda i: (layer, 0, 0))
    out_specs = [pl.BlockSpec((tm, C_W), lambda i: (i, 0))]
    out_shape = [jax.ShapeDtypeStruct((m, C_W), BF16)]
    if emit_v:
        out_specs.append(pl.BlockSpec((tm, C_W), lambda i: (i, 0)))
        out_shape.append(jax.ShapeDtypeStruct((m, C_W), F32))
    return pl.pallas_call(
        functools.partial(_cmlp_kernel, ch=ch),
        grid=(m // tm,),
        in_specs=[
            pl.BlockSpec((tm, C_W), lambda i: (i, COL_Z // C_W)),
            pl.BlockSpec((tm, C_W), lambda i: (i, COL_Z // C_W + 1)),
            vec(), vec(),
            pl.BlockSpec((None, C_GROUPS, CHUNK, CHUNK), lambda i: (layer, 0, 0, 0)),
            pl.BlockSpec((None, CHUNK, C_GROUPS), lambda i: (layer, 0, 0)),
        ],
        out_specs=out_specs,
        out_shape=out_shape,
        compiler_params=_params(("parallel",)),
        name="chunk_mlp",
    )(p, p, ln_g3, ln_b3, w_s, b_st)


def _branch_kernel(oa_ref, ob_ref, oc_ref, g0_ref, g1_ref, g2_ref, wb_ref, m_ref):
    acc = None
    for j, (o_ref, g_ref) in enumerate(((oa_ref, g0_ref), (ob_ref, g1_ref), (oc_ref, g2_ref))):
        y = jnp.dot(o_ref[...], wb_ref[j], preferred_element_type=F32)
        t = g_ref[...].astype(F32) * y
        acc = t if acc is None else acc + t
    m_ref[...] = acc.astype(BF16)


def branch_merge(o_a, o_b, o_c, gates, wb, layer, *, tm, tn):
    m = o_a.shape[0]
    o_spec = lambda: pl.BlockSpec((tm, BRANCH_W), lambda i, j: (i, 0))
    g_spec = lambda b: pl.BlockSpec((tm, tn), lambda i, j: (i, b * D_MODEL // tn + j))
    return pl.pallas_call(
        _branch_kernel,
        grid=(m // tm, D_MODEL // tn),
        in_specs=[o_spec(), o_spec(), o_spec(), g_spec(0), g_spec(1), g_spec(2),
                  pl.BlockSpec((None, N_BRANCH, BRANCH_W, tn), lambda i, j: (layer, 0, 0, j))],
        out_specs=pl.BlockSpec((tm, tn), lambda i, j: (i, j)),
        out_shape=jax.ShapeDtypeStruct((m, D_MODEL), BF16),
        compiler_params=_params(("parallel", "arbitrary")),
        name="branch_merge",
    )(o_a, o_b, o_c, gates, gates, gates, wb)


def _mm_res_kernel(x_ref, w_ref, h_ref, o_ref):
    o_ref[...] = h_ref[...] + jnp.dot(x_ref[...], w_ref[...], preferred_element_type=F32)


def matmul_residual(x, w3, h, layer, *, tm, tn):
    m, k = x.shape
    n = w3.shape[-1]
    return pl.pallas_call(
        _mm_res_kernel,
        grid=(m // tm, n // tn),
        in_specs=[
            pl.BlockSpec((tm, k), lambda i, j: (i, 0)),
            pl.BlockSpec((None, k, tn), lambda i, j: (layer, 0, j)),
            pl.BlockSpec((tm, tn), lambda i, j: (i, j)),
        ],
        out_specs=pl.BlockSpec((tm, tn), lambda i, j: (i, j)),
        out_shape=jax.ShapeDtypeStruct((m, n), F32),
        compiler_params=_params(("parallel", "arbitrary")),
        name="out_proj",
    )(x, w3, h)


def _ffn_kernel(h_ref, g_ref, w1_ref, w2_ref, o_ref, xn_ref):
    @pl.when(pl.program_id(1) == 0)
    def _():
        h = h_ref[...]
        xn_ref[...] = _rms(h, g_ref[...]).astype(BF16)
        o_ref[...] = h

    hid = jnp.maximum(jnp.dot(xn_ref[...], w1_ref[...], preferred_element_type=F32), 0.0)
    o_ref[...] += jnp.dot((hid * hid).astype(BF16), w2_ref[...], preferred_element_type=F32)


def ffn(h, g3, w1, w2, layer, *, tm, tf):
    m, d = h.shape
    return pl.pallas_call(
        _ffn_kernel,
        grid=(m // tm, FF_W // tf),
        in_specs=[
            pl.BlockSpec((tm, d), lambda i, j: (i, 0)),
            pl.BlockSpec((None, 1, d), lambda i, j: (layer, 0, 0)),
            pl.BlockSpec((None, d, tf), lambda i, j: (layer, 0, j)),
            pl.BlockSpec((None, tf, d), lambda i, j: (layer, j, 0)),
        ],
        out_specs=pl.BlockSpec((tm, d), lambda i, j: (i, 0)),
        out_shape=jax.ShapeDtypeStruct((m, d), F32),
        scratch_shapes=[pltpu.VMEM((tm, d), BF16)],
        compiler_params=_params(("parallel", "arbitrary")),
        name="ffn",
    )(h, g3, w1, w2)


def _final_norm_kernel(x_ref, g_ref, o_ref):
    o_ref[...] = _rms(x_ref[...], g_ref[...])


def final_norm(x, g2, *, tm):
    m, d = x.shape
    return pl.pallas_call(
        _final_norm_kernel,
        grid=(m // tm,),
        in_specs=[pl.BlockSpec((tm, d), lambda i: (i, 0)), pl.BlockSpec((1, d), lambda i: (0, 0))],
        out_specs=pl.BlockSpec((tm, d), lambda i: (i, 0)),
        out_shape=jax.ShapeDtypeStruct((m, d), F32),
        compiler_params=_params(("parallel",)),
        name="final_norm",
    )(x, g2)


def _block_diag_gates(w):
    per = 256 // LRU_BW
    w5 = w.reshape(DEPTH, LRU_BLOCKS // per, per, LRU_BW, LRU_BW)
    eye = jnp.eye(per, dtype=w.dtype)
    bd = jnp.einsum("lgbij,bc->lgbicj", w5, eye)
    return bd.reshape(DEPTH, LRU_BLOCKS // per, 256, 256).astype(BF16)


def _trunk_layer(h, layer, wts, attend, conv0, h0, *, tiles, ch, emit_v, kv_stack=None):
    bsz, seq, _ = h.shape
    m = bsz * seq
    h2 = h.reshape(m, D_MODEL)
    if kv_stack is None:
        p, gates = norm_matmul(
            h2, wts["norm1_g"], wts["w_in"], layer, tm=tiles["tm"], tn=tiles["tn_in"])
    else:
        p, gates, *kv_stack = norm_matmul(
            h2, wts["norm1_g"], wts["w_in"], layer, tm=tiles["tm"], tn=tiles["tn_in"],
            caches=tuple(kv_stack) or None, emit_cache=True)
    p3 = p.reshape(bsz, seq, COL_G)
    o_a = attend(p3).reshape(m, BRANCH_W)
    o_b, h_last, conv_tail = rg_lru(
        p3, conv0, h0, wts["conv_w"], wts["conv_b"], wts["w_rg"], wts["w_ig"],
        wts["b_rg"], wts["b_ig"], wts["lru_lambda"], layer, tt=tiles["tt"], tc=256)
    cm = chunk_mlp(p, wts["cmlp_ln_g"], wts["cmlp_ln_b"], wts["w_s"], wts["b_st"], layer,
                   tm=tiles["tm_c"], ch=ch, emit_v=emit_v)
    mrg = branch_merge(o_a, o_b.reshape(m, BRANCH_W), cm[0], gates, wts["w_branch"], layer,
                       tm=tiles["tm"], tn=tiles["tn_b"])
    h2 = matmul_residual(mrg, wts["w_out"], h2, layer, tm=tiles["tm"], tn=tiles["tn"])
    h2 = ffn(h2, wts["norm2_g"], wts["w_ff1"], wts["w_ff2"], layer, tm=tiles["tm_f"], tf=tiles["tf"])
    if kv_stack is None:
        k = p3[..., COL_K:COL_V].reshape(bsz, seq, A_HEADS, 2 * A_HD)
        v = p3[..., COL_V:COL_XB].reshape(bsz, seq, A_HEADS, A_DV)
    else:
        k, v = kv_stack
    new_buf = conv_tail[:, SUBLANES - (CONV_W - 1):]
    v_c = cm[1].reshape(bsz, seq, C_W) if emit_v else None
    return h2.reshape(bsz, seq, D_MODEL), k, v, new_buf, h_last[:, 0], v_c


def kernel(x_prompt, x_sample, cache_k, cache_v, state_h, state_conv, page_table,
           norm1_g, w_in, lam_qk, subln_g, conv_w, conv_b, w_rg, b_rg, w_ig, b_ig, lru_lambda,
           cmlp_ln_g, cmlp_ln_b, w_s, b_s, w_branch, w_out, norm2_g, w_ff1, w_ff2, final_g):
    bp, sp, _ = x_prompt.shape
    bs, ss, _ = x_sample.shape
    n_pool = cache_k.shape[1]
    row = lambda a: a.reshape(DEPTH, 1, a.shape[-1])
    wts = {
        "norm1_g": row(norm1_g), "w_in": w_in.astype(BF16),
        "conv_w": conv_w, "conv_b": row(conv_b),
        "w_rg": _block_diag_gates(w_rg), "w_ig": _block_diag_gates(w_ig),
        "b_rg": row(b_rg), "b_ig": row(b_ig), "lru_lambda": row(lru_lambda),
        "cmlp_ln_g": row(cmlp_ln_g), "cmlp_ln_b": row(cmlp_ln_b),
        "w_s": w_s, "b_st": jnp.swapaxes(b_s, 1, 2),
        "w_branch": w_branch.astype(BF16), "w_out": w_out.astype(BF16),
        "norm2_g": row(norm2_g), "w_ff1": w_ff1.astype(BF16), "w_ff2": w_ff2.astype(BF16),
    }
    subln_g3 = row(subln_g)
    slopes = jnp.exp2(-8.0 * jnp.arange(1, A_HEADS + 1, dtype=F32) / A_HEADS)
    slope_rows = jnp.tile(jnp.repeat(slopes, SUBLANES), 2).reshape(2 * A_HEADS * SUBLANES, 1)
    cache_k4 = cache_k.reshape(DEPTH, n_pool, PAGE_SIZE * A_HEADS, A_DV)
    cache_v4 = cache_v.reshape(DEPTH, n_pool, PAGE_SIZE * A_HEADS, A_DV)
    conv0_p = jnp.zeros((bp, SUBLANES, LRU_W), F32)
    h0_p = jnp.zeros((bp, 1, LRU_W), F32)
    conv0_s = jnp.pad(state_conv, ((0, 0), (0, 0), (SUBLANES - (CONV_W - 1), 0), (0, 0)))

    tiles_p = dict(tm=1024, tn_in=1024, tn_b=512, tn=1024, tt=256, tm_c=512, tm_f=512, tf=1024)
    ms = bs * ss
    tiles_s = dict(tm=ms, tn_in=512, tn_b=512, tn=512, tt=ss, tm_c=ms, tm_f=ms, tf=1024)

    hp, hs = x_prompt, x_sample
    outs = [[] for _ in range(7)]
    kv_stack = ()
    for l in range(DEPTH):
        lam_init = 0.8 - 0.6 * math.exp(-0.3 * l)
        attend_p = functools.partial(
            attention_prompt, slopes=slopes, lam_qk=lam_qk, subln_g3=subln_g3,
            layer=l, lam_init=lam_init, tq=256)
        hp, kp_stack, vp_stack, cp, hlp, _ = _trunk_layer(
            hp, l, wts, attend_p, conv0_p, h0_p, tiles=tiles_p, ch=CHUNK, emit_v=False,
            kv_stack=kv_stack)
        kv_stack = (kp_stack, vp_stack)
        attend_s = functools.partial(
            attention_sample, cache_k4=cache_k4, cache_v4=cache_v4, page_table=page_table,
            slope_rows=slope_rows, lam_qk=lam_qk, subln_g3=subln_g3,
            layer=l, lam_init=lam_init, n_pages=8)
        hs, k_s, v_s, c_s, h_s, vc_s = _trunk_layer(
            hs, l, wts, attend_s, conv0_s[l], state_h[l][:, None, :],
            tiles=tiles_s, ch=min(ss, CHUNK), emit_v=True)
        for lst, val in zip(outs, (hlp, cp, k_s, v_s, h_s, c_s, vc_s)):
            lst.append(val)
    y_prompt = final_norm(hp.reshape(bp * sp, D_MODEL), final_g.reshape(1, D_MODEL), tm=1024)
    y_sample = final_norm(hs.reshape(ms, D_MODEL), final_g.reshape(1, D_MODEL), tm=ms)
    stacked = [jnp.stack(lst) for lst in outs]
    k_prompt, v_prompt = (a.reshape(DEPTH, bp, sp, A_HEADS, A_DV) for a in kv_stack)
    return (y_prompt.reshape(bp, sp, D_MODEL), y_sample.reshape(bs, ss, D_MODEL),
            k_prompt, v_prompt, *stacked)
```

```python
import functools
import math

import jax
import jax.numpy as jnp
from jax import lax
from jax.experimental import pallas as pl
from jax.experimental.pallas import tpu as pltpu

F32 = jnp.float32
BF16 = jnp.bfloat16

D_MODEL = 2048
DEPTH = 4
PAGE_SIZE = 128
BRANCH_W = D_MODEL // 2
A_HD = 64
A_HEADS = BRANCH_W // (2 * A_HD)
A_DV = 2 * A_HD
LRU_W = BRANCH_W
LRU_BLOCKS = 16
LRU_BW = LRU_W // LRU_BLOCKS
CONV_W = 4
LRU_C = 8.0
C_W = BRANCH_W
C_GROUPS = 4
C_GW = C_W // C_GROUPS
CHUNK = 128
N_BRANCH = 3
FF_W = 4 * D_MODEL
EPS = 1e-6

COL_Q = 0
COL_K = BRANCH_W
COL_V = 2 * BRANCH_W
COL_XB = 3 * BRANCH_W
COL_Z = 4 * BRANCH_W
COL_G = 6 * BRANCH_W
W_IN = COL_G + N_BRANCH * D_MODEL

LANES = 128
SUBLANES = 8
NEG_BIG = -1e30
LOG2E = math.log2(math.e)
MIB = 1024 * 1024
VMEM_LIMIT = 52 * MIB


def _params(semantics):
    return pltpu.CompilerParams(dimension_semantics=semantics, vmem_limit_bytes=VMEM_LIMIT)


def _rms(x, g):
    return x * lax.rsqrt(jnp.mean(x * x, axis=-1, keepdims=True) + EPS) * g


def _norm_mm_kernel(x_ref, g_ref, w_ref, *rest, tn, n_passthrough):
    rest = rest[n_passthrough:]
    o_ref, gate_ref, xn_ref = rest[0], rest[1], rest[-1]
    cache_refs = rest[2:-1]
    j = pl.program_id(1)
    n_main = COL_G // tn

    @pl.when(j == 0)
    def _():
        xn_ref[...] = _rms(x_ref[...], g_ref[...]).astype(BF16)

    @pl.when(j < n_main)
    def _():
        o_ref[...] = jnp.dot(xn_ref[...], w_ref[...], preferred_element_type=F32)

    @pl.when(j >= n_main)
    def _():
        pre = jnp.dot(xn_ref[...], w_ref[...], preferred_element_type=F32)
        gate_ref[...] = (0.5 * (jnp.tanh(0.5 * pre) + 1.0)).astype(BF16)

    tm = o_ref.shape[0]
    heads_per_tile = tn // A_DV
    for dst, col0 in zip(cache_refs, (COL_K, COL_V)):
        for t in range(BRANCH_W // tn):
            @pl.when(j == col0 // tn + t)
            def _(dst=dst, t=t):
                for hh in range(heads_per_tile):
                    h = t * heads_per_tile + hh
                    dst[pl.ds(h, tm, stride=A_HEADS), :] = o_ref[:, hh * A_DV:(hh + 1) * A_DV]


def norm_matmul(x, g3, w3, layer, *, tm, tn, caches=None, emit_cache=False):
    m, k = x.shape
    n = w3.shape[-1]
    n_main = COL_G // tn
    in_specs = [
        pl.BlockSpec((tm, k), lambda i, j: (i, 0), pipeline_mode=pl.Buffered(1)),
        pl.BlockSpec((None, 1, k), lambda i, j: (layer, 0, 0)),
        pl.BlockSpec((None, k, tn), lambda i, j: (0, 0, j)),
    ]
    out_specs = [
        pl.BlockSpec((tm, tn), lambda i, j: (i, jnp.minimum(j, n_main - 1))),
        pl.BlockSpec((tm, tn), lambda i, j: (i, jnp.maximum(j - n_main, 0))),
    ]
    out_shape = [jax.ShapeDtypeStruct((m, COL_G), F32),
                 jax.ShapeDtypeStruct((m, n - COL_G), BF16)]
    args = [x, g3, w3]
    aliases = {}
    if emit_cache:
        for c in range(2):
            out_specs.append(pl.BlockSpec((None, tm * A_HEADS, A_DV), lambda i, j: (layer, i, 0),
                                          pipeline_mode=pl.Buffered(1)))
            out_shape.append(jax.ShapeDtypeStruct((DEPTH, m * A_HEADS, A_DV), F32))
            if caches is not None:
                in_specs.append(pl.BlockSpec(memory_space=pl.ANY))
                aliases[len(args)] = 2 + c
                args.append(caches[c])
    kern = functools.partial(_norm_mm_kernel, tn=tn, n_passthrough=len(aliases))
    return pl.pallas_call(
        kern,
        grid=(m // tm, n // tn),
        in_specs=in_specs,
        out_specs=out_specs,
        out_shape=out_shape,
        input_output_aliases=aliases,
        scratch_shapes=[pltpu.VMEM((tm, k), BF16)],
        compiler_params=_params(("parallel", "arbitrary")),
        name="in_proj",
    )(*args)


def _lambda_value(lq, lam_init):
    a = jnp.sum(lq[0:1] * lq[1:2], axis=-1, keepdims=True)
    b = jnp.sum(lq[2:3] * lq[3:4], axis=-1, keepdims=True)
    return jnp.exp(a) - jnp.exp(b) + lam_init


def _online_softmax_update(s, off, vblk, m_ref, l_ref, acc_ref):
    m_old = m_ref[...]
    m_new = jnp.maximum(m_old, jnp.max(s, axis=-1, keepdims=True) + off)
    p = jnp.exp2(s - (m_new - off))
    alpha = jnp.exp2(m_old - m_new)
    l_ref[...] = alpha * l_ref[...] + jnp.sum(p, axis=-1, keepdims=True)
    acc_ref[...] = alpha * acc_ref[...] + jnp.dot(
        p.astype(BF16), vblk, preferred_element_type=F32)
    m_ref[...] = m_new


_NT_DIMS = (((1,), (1,)), ((), ()))


BIAS_SPLIT = 3
ONES_ROWS = 16
HEADS_PER_STEP = 4


def _bf16_pieces(x):
    out = []
    for _ in range(BIAS_SPLIT):
        hi = x.astype(BF16).astype(F32)
        out.append(hi)
        x = x - hi
    return out


def _lane_select(lane, columns):
    out = jnp.zeros_like(columns[0])
    for i, col in enumerate(columns):
        out = jnp.where(lane == i, col, out)
    return out


def _attn_prompt_kernel(slopes_ref, q_ref, k_ref, v_ref, lq_ref, sg_ref, o_ref,
                        kb_ref, vt_ref, qe_ref, mask_ref, m_ref, acc_ref, s_ref, pv_ref, al_ref,
                        *, tq, lam_init):
    hp = pl.program_id(1)
    qi = pl.program_id(2)
    n_blk = vt_ref.shape[1]
    seq = kb_ref.shape[1]
    slopes2 = [slopes_ref[hp * HEADS_PER_STEP + g] * LOG2E for g in range(HEADS_PER_STEP)]

    @pl.when(qi == 0)
    def _():
        lane = lax.broadcasted_iota(jnp.int32, (seq, A_DV), 1)
        kidx = (lax.broadcasted_iota(jnp.int32, (seq, A_DV), 0) & (tq - 1)).astype(F32)
        kside = jnp.where(lane < BIAS_SPLIT, kidx, jnp.where(lane < 2 * BIAS_SPLIT, 1.0, 0.0))
        qlane = lax.broadcasted_iota(jnp.int32, (2 * tq, A_DV), 1)
        qidx = (lax.broadcasted_iota(jnp.int32, (2 * tq, A_DV), 0) & (tq - 1)).astype(F32)
        for g in range(HEADS_PER_STEP):
            cs = slice(g * A_DV, (g + 1) * A_DV)
            kb_ref[g, :, :A_DV] = k_ref[:, cs].astype(BF16)
            kb_ref[g, :, A_DV:] = kside.astype(BF16)
            for jb in range(n_blk):
                vt_ref[g, jb, :A_DV, :] = v_ref[jb * tq:(jb + 1) * tq, cs].T.astype(BF16)
                vt_ref[g, jb, A_DV:, :] = jnp.ones((ONES_ROWS, tq), BF16)
            sl = jnp.full((2 * tq, A_DV), slopes2[g], F32)
            cols = _bf16_pieces(sl) + _bf16_pieces(-(sl * qidx))
            qe_ref[g] = _lane_select(qlane, cols).astype(BF16)
        r = lax.broadcasted_iota(jnp.int32, (tq, 2 * tq), 0)
        c = lax.broadcasted_iota(jnp.int32, (tq, 2 * tq), 1) & (tq - 1)
        mask_ref[...] = jnp.where(r <= c, 0.0, NEG_BIG)

    lane = lax.broadcasted_iota(jnp.int32, (tq, A_DV), 1)
    qs = []
    for g in range(HEADS_PER_STEP):
        q = q_ref[:, g * A_DV:(g + 1) * A_DV] * (A_HD ** -0.5 * LOG2E)
        q2 = jnp.concatenate(
            [jnp.where(lane < A_HD, q, 0.0), jnp.where(lane >= A_HD, q, 0.0)], axis=0)
        qs.append(jnp.concatenate([q2.astype(BF16), qe_ref[g]], axis=1))
    m_ref[...] = jnp.full(m_ref.shape, NEG_BIG, F32)
    acc_ref[...] = jnp.zeros(acc_ref.shape, F32)
    pv_ref[...] = jnp.zeros(pv_ref.shape, F32)
    al_ref[...] = jnp.ones(al_ref.shape, F32)

    def scores(g, j):
        start = pl.multiple_of(j * tq, tq)
        kblk = kb_ref[g, pl.ds(start, tq), :]
        return lax.dot_general(kblk, qs[g], _NT_DIMS, preferred_element_type=F32)

    def softmax_pv(g, j, s, off):
        m_old = m_ref[g]
        m_new = jnp.maximum(m_old, jnp.max(s, axis=0, keepdims=True) + off)
        p = jnp.exp2(s - (m_new - off))
        m_ref[g] = m_new
        pv = jnp.dot(vt_ref[g, j], p.astype(BF16), preferred_element_type=F32)
        return jnp.exp2(m_old - m_new), pv

    for g in range(HEADS_PER_STEP):
        s_ref[0, g] = scores(g, 0)

    def body(j, carry):
        rel = ((j - qi) * tq).astype(F32)
        slot = j & 1
        for g in range(HEADS_PER_STEP):
            s_cur = s_ref[slot, g]
            s_ref[1 - slot, g] = scores(g, j + 1)
            acc_ref[g] = al_ref[g] * acc_ref[g] + pv_ref[g]
            alpha, pv = softmax_pv(g, j, s_cur, slopes2[g] * rel)
            al_ref[g] = alpha
            pv_ref[g] = pv
        return carry

    lax.fori_loop(0, qi, body, 0)
    mask = mask_ref[...]
    lam = _lambda_value(lq_ref[...], lam_init)
    for g in range(HEADS_PER_STEP):
        acc = al_ref[g] * acc_ref[g] + pv_ref[g]
        alpha, pv = softmax_pv(g, qi, s_ref[qi & 1, g] + mask, 0.0)
        acc = alpha * acc + pv
        on = acc[:A_DV] / acc[A_DV:A_DV + 1]
        ot = on[:, :tq] - lam * on[:, tq:]
        yt = ot * lax.rsqrt(jnp.mean(ot * ot, axis=0, keepdims=True) + EPS)
        o_ref[:, g * A_DV:(g + 1) * A_DV] = (
            (yt.T * sg_ref[...]) * (1.0 - lam_init)).astype(BF16)


def attention_prompt(p3, slopes, lam_qk, subln_g3, layer, lam_init, *, tq):
    b, s, _ = p3.shape
    gw = HEADS_PER_STEP * A_DV
    kern = functools.partial(_attn_prompt_kernel, tq=tq, lam_init=lam_init)
    return pl.pallas_call(
        kern,
        grid=(b, A_HEADS // HEADS_PER_STEP, s // tq),
        in_specs=[
            pl.BlockSpec(memory_space=pltpu.SMEM),
            pl.BlockSpec((None, tq, gw), lambda bi, h, qi: (bi, qi, COL_Q // gw + h)),
            pl.BlockSpec((None, s, gw), lambda bi, h, qi: (bi, 0, COL_K // gw + h)),
            pl.BlockSpec((None, s, gw), lambda bi, h, qi: (bi, 0, COL_V // gw + h)),
            pl.BlockSpec((None, 4, A_HD), lambda bi, h, qi: (layer, 0, 0)),
            pl.BlockSpec((None, 1, A_DV), lambda bi, h, qi: (layer, 0, 0)),
        ],
        out_specs=pl.BlockSpec((None, tq, gw), lambda bi, h, qi: (bi, qi, h)),
        out_shape=jax.ShapeDtypeStruct((b, s, BRANCH_W), BF16),
        scratch_shapes=[
            pltpu.VMEM((HEADS_PER_STEP, s, 2 * A_DV), BF16),
            pltpu.VMEM((HEADS_PER_STEP, s // tq, A_DV + ONES_ROWS, tq), BF16),
            pltpu.VMEM((HEADS_PER_STEP, 2 * tq, A_DV), BF16),
            pltpu.VMEM((tq, 2 * tq), F32),
            pltpu.VMEM((HEADS_PER_STEP, 1, 2 * tq), F32),
            pltpu.VMEM((HEADS_PER_STEP, A_DV + ONES_ROWS, 2 * tq), F32),
            pltpu.VMEM((2, HEADS_PER_STEP, tq, 2 * tq), F32),
            pltpu.VMEM((HEADS_PER_STEP, A_DV + ONES_ROWS, 2 * tq), F32),
            pltpu.VMEM((HEADS_PER_STEP, 1, 2 * tq), F32),
        ],
        compiler_params=_params(("parallel", "arbitrary", "arbitrary")),
        name="attn_prompt",
    )(slopes, p3, p3, p3, lam_qk, subln_g3)


def _attn_sample_kernel(pt_ref, q_ref, kn_ref, vn_ref, slope_ref, lq_ref, sg_ref, *rest,
                        n_pages, n_steps, past_len, lam_init):
    del pt_ref
    kp_refs = rest[:n_pages]
    vp_refs = rest[n_pages:2 * n_pages]
    o_ref, qall_ref, sb_ref, m_ref, l_ref, acc_ref = rest[2 * n_pages:]
    s_idx = pl.program_id(1)
    n_rows = 2 * A_HEADS * SUBLANES
    half = A_HEADS * SUBLANES
    span = n_pages * PAGE_SIZE
    slope = slope_ref[...] * LOG2E

    def head_cols(ref, h):
        return ref[:, h * A_DV:(h + 1) * A_DV]

    @pl.when(s_idx == 0)
    def _():
        lane = lax.broadcasted_iota(jnp.int32, (SUBLANES, A_DV), 1)
        pieces = []
        for mp in range(2):
            keep = (lane >= A_HD) if mp else (lane < A_HD)
            for h in range(A_HEADS):
                pieces.append(
                    jnp.where(keep, head_cols(q_ref, h) * (A_HD ** -0.5 * LOG2E), 0.0))
        qall_ref[...] = jnp.concatenate(pieces, axis=0).astype(BF16)
        r = lax.broadcasted_iota(jnp.int32, sb_ref.shape, 0)
        c = lax.broadcasted_iota(jnp.int32, sb_ref.shape, 1)
        bias = slope * ((c >> 3) - (r & 7)).astype(F32)
        sb_ref[...] = jnp.where((c & 7) == ((r >> 3) & 7), bias, NEG_BIG)
        m_ref[...] = jnp.full(m_ref.shape, NEG_BIG, F32)
        l_ref[...] = jnp.zeros(l_ref.shape, F32)
        acc_ref[...] = jnp.zeros(acc_ref.shape, F32)

    qall = qall_ref[...]
    kcat = jnp.concatenate([r[...].astype(BF16) for r in kp_refs], axis=0)
    vcat = jnp.concatenate([r[...].astype(BF16) for r in vp_refs], axis=0)
    s = lax.dot_general(qall, kcat, _NT_DIMS, preferred_element_type=F32) + sb_ref[...]
    off = slope * (s_idx * span - past_len).astype(F32)
    _online_softmax_update(s, off, vcat, m_ref, l_ref, acc_ref)

    @pl.when(s_idx == n_steps - 1)
    def _():
        pad = jnp.zeros((PAGE_SIZE - half, A_DV), F32)
        kn = jnp.concatenate([head_cols(kn_ref, h) for h in range(A_HEADS)] + [pad], axis=0)
        vn = jnp.concatenate([head_cols(vn_ref, h) for h in range(A_HEADS)] + [pad], axis=0)
        sn = lax.dot_general(qall, kn.astype(BF16), _NT_DIMS, preferred_element_type=F32)
        r = lax.broadcasted_iota(jnp.int32, sn.shape, 0)
        c = lax.broadcasted_iota(jnp.int32, sn.shape, 1)
        qpos = r & 7
        kpos = c & 7
        valid = (c < half) & ((c >> 3) == ((r >> 3) & 7)) & (kpos <= qpos)
        sn = jnp.where(valid, sn + slope * (kpos - qpos).astype(F32), NEG_BIG)
        _online_softmax_update(sn, 0.0, vn.astype(BF16), m_ref, l_ref, acc_ref)

        o_map = acc_ref[...] / l_ref[...]
        lam = _lambda_value(lq_ref[...], lam_init)
        outs = []
        for h in range(A_HEADS):
            o = (o_map[h * SUBLANES:(h + 1) * SUBLANES]
                 - lam * o_map[half + h * SUBLANES:half + (h + 1) * SUBLANES])
            outs.append(_rms(o, sg_ref[...]) * (1.0 - lam_init))
        o_ref[...] = jnp.concatenate(outs, axis=1).astype(BF16)


def attention_sample(p3, cache_k4, cache_v4, page_table, slope_rows, lam_qk, subln_g3,
                     layer, lam_init, *, n_pages):
    b, t, _ = p3.shape
    assert t == SUBLANES
    pages_total = page_table.shape[1]
    n_steps = pages_total // n_pages
    n_rows = 2 * A_HEADS * SUBLANES
    page_rows = PAGE_SIZE * A_HEADS
    kern = functools.partial(
        _attn_sample_kernel, n_pages=n_pages, n_steps=n_steps,
        past_len=pages_total * PAGE_SIZE, lam_init=lam_init)

    def page_spec(i):
        return pl.BlockSpec(
            (None, None, page_rows, A_DV),
            lambda bi, si, pt: (layer, pt[bi, si * n_pages + i], 0, 0))

    col = lambda c: pl.BlockSpec((None, t, BRANCH_W), lambda bi, si, pt: (bi, 0, c // BRANCH_W))
    grid_spec = pltpu.PrefetchScalarGridSpec(
        num_scalar_prefetch=1,
        grid=(b, n_steps),
        in_specs=[
            col(COL_Q), col(COL_K), col(COL_V),
            pl.BlockSpec((n_rows, 1), lambda bi, si, pt: (0, 0)),
            pl.BlockSpec((None, 4, A_HD), lambda bi, si, pt: (layer, 0, 0)),
            pl.BlockSpec((None, 1, A_DV), lambda bi, si, pt: (layer, 0, 0)),
        ] + [page_spec(i) for i in range(n_pages)] * 2,
        out_specs=pl.BlockSpec((None, t, BRANCH_W), lambda bi, si, pt: (bi, 0, 0)),
        scratch_shapes=[
            pltpu.VMEM((n_rows, A_DV), BF16),
            pltpu.VMEM((n_rows, n_pages * page_rows), F32),
            pltpu.VMEM((n_rows, 1), F32),
            pltpu.VMEM((n_rows, 1), F32),
            pltpu.VMEM((n_rows, A_DV), F32),
        ],
    )
    return pl.pallas_call(
        kern,
        grid_spec=grid_spec,
        out_shape=jax.ShapeDtypeStruct((b, t, BRANCH_W), BF16),
        compiler_params=_params(("parallel", "arbitrary")),
        name="attn_sample",
    )(page_table, p3, p3, p3, slope_rows, lam_qk, subln_g3,
      *([cache_k4] * n_pages), *([cache_v4] * n_pages))


def _shift_rows(x, prev, j, row8):
    xr = pltpu.roll(x, j, 0)
    head = jnp.where(row8 < j, pltpu.roll(prev, j, 0), xr[:SUBLANES])
    if x.shape[0] == SUBLANES:
        return head
    return jnp.concatenate([head, xr[SUBLANES:]], axis=0)


def _lru_kernel(x_ref, c0_ref, h0_ref, cw_ref, cb_ref, wr_ref, wi_ref, br_ref, bi_ref, lam_ref,
                hs_ref, hl_ref, co_ref, hprev_ref, xprev_ref):
    t_idx = pl.program_id(2)

    @pl.when(t_idx == 0)
    def _():
        hprev_ref[...] = h0_ref[...]
        xprev_ref[...] = c0_ref[...]

    x = x_ref[...]
    n_t, tc = x.shape
    prev = xprev_ref[...]
    row8 = lax.broadcasted_iota(jnp.int32, (SUBLANES, tc), 0)
    cw = cw_ref[...]
    xc = cb_ref[...] + cw[0:1] * _shift_rows(x, prev, 3, row8)
    xc = xc + cw[1:2] * _shift_rows(x, prev, 2, row8)
    xc = xc + cw[2:3] * _shift_rows(x, prev, 1, row8)
    xc = xc + cw[3:4] * x
    tail = x[n_t - SUBLANES:]
    xprev_ref[...] = tail
    co_ref[...] = tail

    xcb = xc.astype(BF16)
    r = jax.nn.sigmoid(jnp.dot(xcb, wr_ref[...], preferred_element_type=F32) + br_ref[...])
    i = jax.nn.sigmoid(jnp.dot(xcb, wi_ref[...], preferred_element_type=F32) + bi_ref[...])
    nl = -lam_ref[...]
    softplus = jnp.maximum(nl, 0.0) + jnp.log1p(jnp.exp(-jnp.abs(nl)))
    log_a = (-LRU_C) * r * softplus
    a = jnp.exp(log_a)
    th = jnp.tanh(log_a)
    u = jnp.sqrt(-2.0 * th / (1.0 - th)) * (i * xc)

    n_g = n_t // SUBLANES
    a = a.reshape(n_g, SUBLANES, tc)
    u = u.reshape(n_g, SUBLANES, tc)
    row = lax.broadcasted_iota(jnp.int32, a.shape, 1)
    d = 1
    while d < SUBLANES:
        a_sh = jnp.where(row >= d, pltpu.roll(a, d, 1), 1.0)
        u_sh = jnp.where(row >= d, pltpu.roll(u, d, 1), 0.0)
        u = u + a * u_sh
        a = a * a_sh
        d *= 2
    carry = hprev_ref[...]
    groups = []
    for g in range(n_g):
        groups.append(u[g] + a[g] * carry)
        carry = groups[-1][SUBLANES - 1:]
    hs_ref[...] = jnp.concatenate(groups, axis=0).astype(BF16)
    hprev_ref[...] = carry
    hl_ref[...] = carry


def rg_lru(p3, conv0, h0, conv_w, conv_b3, wr_bd, wi_bd, b_rg3, b_ig3, lam3, layer, *, tt, tc):
    b, s, _ = p3.shape
    c = LRU_W
    vec = lambda: pl.BlockSpec((None, 1, tc), lambda bi, ci, ti: (layer, 0, ci))
    gate_w = lambda: pl.BlockSpec((None, None, tc, tc), lambda bi, ci, ti: (layer, ci, 0, 0))
    state = lambda rows: pl.BlockSpec((None, rows, tc), lambda bi, ci, ti: (bi, 0, ci))
    return pl.pallas_call(
        _lru_kernel,
        grid=(b, c // tc, s // tt),
        in_specs=[
            pl.BlockSpec((None, tt, tc), lambda bi, ci, ti: (bi, ti, COL_XB // tc + ci)),
            state(SUBLANES), state(1),
            pl.BlockSpec((None, CONV_W, tc), lambda bi, ci, ti: (layer, 0, ci)),
            vec(), gate_w(), gate_w(), vec(), vec(), vec(),
        ],
        out_specs=[
            pl.BlockSpec((None, tt, tc), lambda bi, ci, ti: (bi, ti, ci)),
            state(1), state(SUBLANES),
        ],
        out_shape=[
            jax.ShapeDtypeStruct((b, s, c), BF16),
            jax.ShapeDtypeStruct((b, 1, c), F32),
            jax.ShapeDtypeStruct((b, SUBLANES, c), F32),
        ],
        scratch_shapes=[pltpu.VMEM((1, tc), F32), pltpu.VMEM((SUBLANES, tc), F32)],
        compiler_params=_params(("parallel", "parallel", "arbitrary")),
        name="rg_lru",
    )(p3, conv0, h0, conv_w, conv_b3, wr_bd, wi_bd, b_rg3, b_ig3, lam3)


def _gelu_tanh(x):
    c = math.sqrt(2.0 / math.pi)
    return 0.5 * x * (1.0 + jnp.tanh(c * (x + 0.044715 * (x * x * x))))


def _cmlp_kernel(zu_ref, zv_ref, lg_ref, lb_ref, ws_ref, bst_ref, oc_ref, *maybe_vc_ref, ch):
    u = _gelu_tanh(zu_ref[...])
    vg = _gelu_tanh(zv_ref[...])
    xc = vg - jnp.mean(vg, axis=-1, keepdims=True)
    v = xc * lax.rsqrt(jnp.mean(xc * xc, axis=-1, keepdims=True) + EPS) * lg_ref[...] + lb_ref[...]
    if maybe_vc_ref:
        maybe_vc_ref[0][...] = v
    n_ch = u.shape[0] // ch
    r = lax.broadcasted_iota(jnp.int32, (ch, ch), 0)
    c = lax.broadcasted_iota(jnp.int32, (ch, ch), 1)
    for g in range(C_GROUPS):
        wm = jnp.where(c <= r, ws_ref[g][:ch, :ch], 0.0)
        bias = bst_ref[:ch, g:g + 1]
        cs = slice(g * C_GW, (g + 1) * C_GW)
        for n in range(n_ch):
            rs = slice(n * ch, (n + 1) * ch)
            vb = v[rs, cs]
            if ch >= LANES:
                f = jnp.dot(wm.astype(BF16), vb.astype(BF16), preferred_element_type=F32) + bias
            else:
                f = bias + wm[:, 0:1] * vb[0:1, :]
                for s in range(1, ch):
                    f = f + wm[:, s:s + 1] * vb[s:s + 1, :]
            oc_ref[rs, cs] = (u[rs, cs] * f).astype(BF16)


def chunk_mlp(p, ln_g3, ln_b3, w_s, b_st, layer, *, tm, ch, emit_v):
    m = p.shape[0]
    vec = lambda: pl.BlockSpec((None, 1, C_W), lambda i: (layer, 0, 0))
    out_specs = [pl.BlockSpec((tm, C_W), lambda i: (i, 0))]
    out_shape = [jax.ShapeDtypeStruct((m, C_W), BF16)]
    if emit_v:
        out_specs.append(pl.BlockSpec((tm, C_W), lambda i: (i, 0)))
        out_shape.append(jax.ShapeDtypeStruct((m, C_W), F32))
    return pl.pallas_call(
        functools.partial(_cmlp_kernel, ch=ch),
        grid=(m // tm,),
        in_specs=[
            pl.BlockSpec((tm, C_W), lambda i: (i, COL_Z // C_W)),
            pl.BlockSpec((tm, C_W), lambda i: (i, COL_Z // C_W + 1)),
            vec(), vec(),
            pl.BlockSpec((None, C_GROUPS, CHUNK, CHUNK), lambda i: (layer, 0, 0, 0)),
            pl.BlockSpec((None, CHUNK, C_GROUPS), lambda i: (layer, 0, 0)),
        ],
        out_specs=out_specs,
        out_shape=out_shape,
        compiler_params=_params(("parallel",)),
        name="chunk_mlp",
    )(p, p, ln_g3, ln_b3, w_s, b_st)


def _branch_kernel(oa_ref, ob_ref, oc_ref, g0_ref, g1_ref, g2_ref, wb_ref, m_ref):
    acc = None
    for j, (o_ref, g_ref) in enumerate(((oa_ref, g0_ref), (ob_ref, g1_ref), (oc_ref, g2_ref))):
        y = jnp.dot(o_ref[...], wb_ref[j], preferred_element_type=F32)
        t = g_ref[...].astype(F32) * y
        acc = t if acc is None else acc + t
    m_ref[...] = acc.astype(BF16)


def branch_merge(o_a, o_b, o_c, gates, wb, *, tm, tn):
    m = o_a.shape[0]
    o_spec = lambda: pl.BlockSpec((tm, BRANCH_W), lambda i, j: (i, 0))
    g_spec = lambda b: pl.BlockSpec((tm, tn), lambda i, j: (i, b * D_MODEL // tn + j))
    return pl.pallas_call(
        _branch_kernel,
        grid=(m // tm, D_MODEL // tn),
        in_specs=[o_spec(), o_spec(), o_spec(), g_spec(0), g_spec(1), g_spec(2),
                  pl.BlockSpec((None, N_BRANCH, BRANCH_W, tn), lambda i, j: (0, 0, 0, j))],
        out_specs=pl.BlockSpec((tm, tn), lambda i, j: (i, j)),
        out_shape=jax.ShapeDtypeStruct((m, D_MODEL), BF16),
        compiler_params=_params(("parallel", "arbitrary")),
        name="branch_merge",
    )(o_a, o_b, o_c, gates, gates, gates, wb)


def _mm_res_kernel(x_ref, w_ref, h_ref, o_ref):
    o_ref[...] = h_ref[...] + jnp.dot(x_ref[...], w_ref[...], preferred_element_type=F32)


def matmul_residual(x, w3, h, *, tm, tn):
    m, k = x.shape
    n = w3.shape[-1]
    return pl.pallas_call(
        _mm_res_kernel,
        grid=(m // tm, n // tn),
        in_specs=[
            pl.BlockSpec((tm, k), lambda i, j: (i, 0)),
            pl.BlockSpec((None, k, tn), lambda i, j: (0, 0, j)),
            pl.BlockSpec((tm, tn), lambda i, j: (i, j)),
        ],
        out_specs=pl.BlockSpec((tm, tn), lambda i, j: (i, j)),
        out_shape=jax.ShapeDtypeStruct((m, n), F32),
        compiler_params=_params(("parallel", "arbitrary")),
        name="out_proj",
    )(x, w3, h)


def _ffn_kernel(h_ref, g_ref, w1_ref, w2_ref, *rest, n_cast):
    src_refs = rest[:n_cast]
    o_ref = rest[n_cast]
    dst_refs = rest[n_cast + 1:2 * n_cast + 1]
    xn_ref = rest[-1]

    @pl.when(pl.program_id(1) == 0)
    def _():
        h = h_ref[...]
        xn_ref[...] = _rms(h, g_ref[...]).astype(BF16)
        o_ref[...] = h

    hid = jnp.maximum(jnp.dot(xn_ref[...], w1_ref[...], preferred_element_type=F32), 0.0)
    o_ref[...] += jnp.dot((hid * hid).astype(BF16), w2_ref[...], preferred_element_type=F32)
    for src, dst in zip(src_refs, dst_refs):
        dst[...] = src[...].astype(BF16)


BF16_SUBLANES = 16


def ffn(h, g3, w1, w2, layer, *, tm, tf, cast_next=()):
    m, d = h.shape
    n_i, n_j = m // tm, FF_W // tf
    n_steps = n_i * n_j
    in_specs = [
        pl.BlockSpec((tm, d), lambda i, j: (i, 0)),
        pl.BlockSpec((None, 1, d), lambda i, j: (layer, 0, 0)),
        pl.BlockSpec((None, d, tf), lambda i, j: (0, 0, j)),
        pl.BlockSpec((None, tf, d), lambda i, j: (0, j, 0)),
    ]
    out_specs = [pl.BlockSpec((tm, d), lambda i, j: (i, 0))]
    out_shape = [jax.ShapeDtypeStruct((m, d), F32)]
    for arr, src_layer in cast_next:
        _, r, c = arr.shape
        rows = pl.cdiv(pl.cdiv(r, n_steps), BF16_SUBLANES) * BF16_SUBLANES
        n_slabs = r // rows
        assert n_slabs * rows == r and n_slabs <= n_steps
        slab = lambda i, j, n_slabs=n_slabs: jnp.minimum(i * n_j + j, n_slabs - 1)
        in_specs.append(pl.BlockSpec(
            (None, rows, c), lambda i, j, slab=slab, sl=src_layer: (sl, slab(i, j), 0)))
        out_specs.append(pl.BlockSpec((None, rows, c), lambda i, j, slab=slab: (0, slab(i, j), 0)))
        out_shape.append(jax.ShapeDtypeStruct((1, r, c), BF16))
    outs = pl.pallas_call(
        functools.partial(_ffn_kernel, n_cast=len(cast_next)),
        grid=(n_i, n_j),
        in_specs=in_specs,
        out_specs=out_specs,
        out_shape=out_shape,
        scratch_shapes=[pltpu.VMEM((tm, d), BF16)],
        compiler_params=_params(("parallel", "arbitrary")),
        name="ffn",
    )(h, g3, w1, w2, *[arr for arr, _ in cast_next])
    return outs[0], outs[1:]


def _final_norm_kernel(x_ref, g_ref, o_ref):
    o_ref[...] = _rms(x_ref[...], g_ref[...])


def final_norm(x, g2, *, tm):
    m, d = x.shape
    return pl.pallas_call(
        _final_norm_kernel,
        grid=(m // tm,),
        in_specs=[pl.BlockSpec((tm, d), lambda i: (i, 0)), pl.BlockSpec((1, d), lambda i: (0, 0))],
        out_specs=pl.BlockSpec((tm, d), lambda i: (i, 0)),
        out_shape=jax.ShapeDtypeStruct((m, d), F32),
        compiler_params=_params(("parallel",)),
        name="final_norm",
    )(x, g2)


def _block_diag_gates(w):
    per = 256 // LRU_BW
    w5 = w.reshape(DEPTH, LRU_BLOCKS // per, per, LRU_BW, LRU_BW)
    eye = jnp.eye(per, dtype=w.dtype)
    bd = jnp.einsum("lgbij,bc->lgbicj", w5, eye)
    return bd.reshape(DEPTH, LRU_BLOCKS // per, 256, 256).astype(BF16)


def _trunk_layer(h, layer, wts, lw, attend, conv0, h0, *, tiles, ch, emit_v, kv_stack=None,
                 cast_next=()):
    bsz, seq, _ = h.shape
    m = bsz * seq
    h2 = h.reshape(m, D_MODEL)
    if kv_stack is None:
        p, gates = norm_matmul(
            h2, wts["norm1_g"], lw["w_in"], layer, tm=tiles["tm"], tn=tiles["tn_in"])
    else:
        p, gates, *kv_stack = norm_matmul(
            h2, wts["norm1_g"], lw["w_in"], layer, tm=tiles["tm"], tn=tiles["tn_in"],
            caches=tuple(kv_stack) or None, emit_cache=True)
    p3 = p.reshape(bsz, seq, COL_G)
    o_a = attend(p3).reshape(m, BRANCH_W)
    o_b, h_last, conv_tail = rg_lru(
        p3, conv0, h0, wts["conv_w"], wts["conv_b"], wts["w_rg"], wts["w_ig"],
        wts["b_rg"], wts["b_ig"], wts["lru_lambda"], layer, tt=tiles["tt"], tc=256)
    cm = chunk_mlp(p, wts["cmlp_ln_g"], wts["cmlp_ln_b"], wts["w_s"], wts["b_st"], layer,
                   tm=tiles["tm_c"], ch=ch, emit_v=emit_v)
    mrg = branch_merge(o_a, o_b.reshape(m, BRANCH_W), cm[0], gates, lw["w_branch"],
                       tm=tiles["tm"], tn=tiles["tn_b"])
    h2 = matmul_residual(mrg, lw["w_out"], h2, tm=tiles["tm"], tn=tiles["tn"])
    h2, next_weights = ffn(h2, wts["norm2_g"], lw["w_ff1"], lw["w_ff2"], layer,
                           tm=tiles["tm_f"], tf=tiles["tf"], cast_next=cast_next)
    if kv_stack is None:
        k = p3[..., COL_K:COL_V].reshape(bsz, seq, A_HEADS, 2 * A_HD)
        v = p3[..., COL_V:COL_XB].reshape(bsz, seq, A_HEADS, A_DV)
    else:
        k, v = kv_stack
    new_buf = conv_tail[:, SUBLANES - (CONV_W - 1):]
    v_c = cm[1].reshape(bsz, seq, C_W) if emit_v else None
    return h2.reshape(bsz, seq, D_MODEL), k, v, new_buf, h_last[:, 0], v_c, next_weights


def kernel(x_prompt, x_sample, cache_k, cache_v, state_h, state_conv, page_table,
           norm1_g, w_in, lam_qk, subln_g, conv_w, conv_b, w_rg, b_rg, w_ig, b_ig, lru_lambda,
           cmlp_ln_g, cmlp_ln_b, w_s, b_s, w_branch, w_out, norm2_g, w_ff1, w_ff2, final_g):
    bp, sp, _ = x_prompt.shape
    bs, ss, _ = x_sample.shape
    n_pool = cache_k.shape[1]
    row = lambda a: a.reshape(DEPTH, 1, a.shape[-1])
    wts = {
        "norm1_g": row(norm1_g),
        "conv_w": conv_w, "conv_b": row(conv_b),
        "w_rg": _block_diag_gates(w_rg), "w_ig": _block_diag_gates(w_ig),
        "b_rg": row(b_rg), "b_ig": row(b_ig), "lru_lambda": row(lru_lambda),
        "cmlp_ln_g": row(cmlp_ln_g), "cmlp_ln_b": row(cmlp_ln_b),
        "w_s": w_s, "b_st": jnp.swapaxes(b_s, 1, 2),
        "norm2_g": row(norm2_g),
    }
    weight_names = ("w_in", "w_branch", "w_out", "w_ff1", "w_ff2")
    weight_views = (w_in, w_branch.reshape(DEPTH, N_BRANCH * BRANCH_W, D_MODEL), w_out, w_ff1, w_ff2)

    def layer_weights(arrs):
        lw = dict(zip(weight_names, arrs))
        lw["w_branch"] = lw["w_branch"].reshape(1, N_BRANCH, BRANCH_W, D_MODEL)
        return lw

    lw = layer_weights([v[0:1].astype(BF16) for v in weight_views])
    subln_g3 = row(subln_g)
    slopes = jnp.exp2(-8.0 * jnp.arange(1, A_HEADS + 1, dtype=F32) / A_HEADS)
    slope_rows = jnp.tile(jnp.repeat(slopes, SUBLANES), 2).reshape(2 * A_HEADS * SUBLANES, 1)
    cache_k4 = cache_k.reshape(DEPTH, n_pool, PAGE_SIZE * A_HEADS, A_DV)
    cache_v4 = cache_v.reshape(DEPTH, n_pool, PAGE_SIZE * A_HEADS, A_DV)
    conv0_p = jnp.zeros((bp, SUBLANES, LRU_W), F32)
    h0_p = jnp.zeros((bp, 1, LRU_W), F32)
    conv0_s = jnp.pad(state_conv, ((0, 0), (0, 0), (SUBLANES - (CONV_W - 1), 0), (0, 0)))

    tiles_p = dict(tm=1024, tn_in=1024, tn_b=512, tn=1024, tt=256, tm_c=512, tm_f=512, tf=1024)
    ms = bs * ss
    tiles_s = dict(tm=ms, tn_in=512, tn_b=512, tn=512, tt=ss, tm_c=ms, tm_f=ms, tf=1024)

    hp, hs = x_prompt, x_sample
    outs = [[] for _ in range(7)]
    kv_stack = ()
    for l in range(DEPTH):
        lam_init = 0.8 - 0.6 * math.exp(-0.3 * l)
        attend_p = functools.partial(
            attention_prompt, slopes=slopes, lam_qk=lam_qk, subln_g3=subln_g3,
            layer=l, lam_init=lam_init, tq=256)
        cast_next = tuple((v, l + 1) for v in weight_views) if l + 1 < DEPTH else ()
        hp, kp_stack, vp_stack, cp, hlp, _, next_weights = _trunk_layer(
            hp, l, wts, lw, attend_p, conv0_p, h0_p, tiles=tiles_p, ch=CHUNK, emit_v=False,
            kv_stack=kv_stack, cast_next=cast_next)
        kv_stack = (kp_stack, vp_stack)
        attend_s = functools.partial(
            attention_sample, cache_k4=cache_k4, cache_v4=cache_v4, page_table=page_table,
            slope_rows=slope_rows, lam_qk=lam_qk, subln_g3=subln_g3,
            layer=l, lam_init=lam_init, n_pages=8)
        hs, k_s, v_s, c_s, h_s, vc_s, _ = _trunk_layer(
            hs, l, wts, lw, attend_s, conv0_s[l], state_h[l][:, None, :],
            tiles=tiles_s, ch=min(ss, CHUNK), emit_v=True)
        if next_weights:
            lw = layer_weights(next_weights)
        for lst, val in zip(outs, (hlp, cp, k_s, v_s, h_s, c_s, vc_s)):
            lst.append(val)
    y_prompt = final_norm(hp.reshape(bp * sp, D_MODEL), final_g.reshape(1, D_MODEL), tm=1024)
    y_sample = final_norm(hs.reshape(ms, D_MODEL), final_g.reshape(1, D_MODEL), tm=ms)
    stacked = [jnp.stack(lst) for lst in outs]
    k_prompt, v_prompt = (a.reshape(DEPTH, bp, sp, A_HEADS, A_DV) for a in kv_stack)
    return (y_prompt.reshape(bp, sp, D_MODEL), y_sample.reshape(bs, ss, D_MODEL),
            k_prompt, v_prompt, *stacked)
```

```python
import functools
import math

import jax
import jax.numpy as jnp
from jax import lax
from jax.experimental import pallas as pl
from jax.experimental.pallas import tpu as pltpu

F32 = jnp.float32
BF16 = jnp.bfloat16

D_MODEL = 2048
DEPTH = 4
PAGE_SIZE = 128
BRANCH_W = D_MODEL // 2
A_HD = 64
A_HEADS = BRANCH_W // (2 * A_HD)
A_DV = 2 * A_HD
LRU_W = BRANCH_W
LRU_BLOCKS = 16
LRU_BW = LRU_W // LRU_BLOCKS
CONV_W = 4
LRU_C = 8.0
C_W = BRANCH_W
C_GROUPS = 4
C_GW = C_W // C_GROUPS
CHUNK = 128
N_BRANCH = 3
FF_W = 4 * D_MODEL
EPS = 1e-6

COL_Q = 0
COL_K = BRANCH_W
COL_V = 2 * BRANCH_W
COL_XB = 3 * BRANCH_W
COL_Z = 4 * BRANCH_W
COL_G = 6 * BRANCH_W
W_IN = COL_G + N_BRANCH * D_MODEL

LANES = 128
SUBLANES = 8
NEG_BIG = -1e30
LOG2E = math.log2(math.e)
MIB = 1024 * 1024
VMEM_LIMIT = 52 * MIB


def _params(semantics):
    return pltpu.CompilerParams(dimension_semantics=semantics, vmem_limit_bytes=VMEM_LIMIT)


def _rms(x, g):
    return x * lax.rsqrt(jnp.mean(x * x, axis=-1, keepdims=True) + EPS) * g


def _norm_mm_kernel(x_ref, g_ref, w_ref, *rest, tn, n_passthrough):
    rest = rest[n_passthrough:]
    o_ref, gate_ref, xn_ref = rest[0], rest[1], rest[-1]
    cache_refs = rest[2:-1]
    j = pl.program_id(1)
    n_main = COL_G // tn

    @pl.when(j == 0)
    def _():
        xn_ref[...] = _rms(x_ref[...], g_ref[...]).astype(BF16)

    @pl.when(j < n_main)
    def _():
        o_ref[...] = jnp.dot(xn_ref[...], w_ref[...], preferred_element_type=F32)

    @pl.when(j >= n_main)
    def _():
        pre = jnp.dot(xn_ref[...], w_ref[...], preferred_element_type=F32)
        gate_ref[...] = (0.5 * (jnp.tanh(0.5 * pre) + 1.0)).astype(BF16)

    tm = o_ref.shape[0]
    heads_per_tile = tn // A_DV
    for dst, col0 in zip(cache_refs, (COL_K, COL_V)):
        for t in range(BRANCH_W // tn):
            @pl.when(j == col0 // tn + t)
            def _(dst=dst, t=t):
                for hh in range(heads_per_tile):
                    h = t * heads_per_tile + hh
                    dst[pl.ds(h, tm, stride=A_HEADS), :] = o_ref[:, hh * A_DV:(hh + 1) * A_DV]


def norm_matmul(x, g3, w3, layer, *, tm, tn, caches=None, emit_cache=False):
    m, k = x.shape
    n = w3.shape[-1]
    n_main = COL_G // tn
    in_specs = [
        pl.BlockSpec((tm, k), lambda i, j: (i, 0), pipeline_mode=pl.Buffered(1)),
        pl.BlockSpec((None, 1, k), lambda i, j: (layer, 0, 0)),
        pl.BlockSpec((None, k, tn), lambda i, j: (0, 0, j)),
    ]
    out_specs = [
        pl.BlockSpec((tm, tn), lambda i, j: (i, jnp.minimum(j, n_main - 1))),
        pl.BlockSpec((tm, tn), lambda i, j: (i, jnp.maximum(j - n_main, 0))),
    ]
    out_shape = [jax.ShapeDtypeStruct((m, COL_G), F32),
                 jax.ShapeDtypeStruct((m, n - COL_G), BF16)]
    args = [x, g3, w3]
    aliases = {}
    if emit_cache:
        for c in range(2):
            out_specs.append(pl.BlockSpec((None, tm * A_HEADS, A_DV), lambda i, j: (layer, i, 0),
                                          pipeline_mode=pl.Buffered(1)))
            out_shape.append(jax.ShapeDtypeStruct((DEPTH, m * A_HEADS, A_DV), F32))
            if caches is not None:
                in_specs.append(pl.BlockSpec(memory_space=pl.ANY))
                aliases[len(args)] = 2 + c
                args.append(caches[c])
    kern = functools.partial(_norm_mm_kernel, tn=tn, n_passthrough=len(aliases))
    return pl.pallas_call(
        kern,
        grid=(m // tm, n // tn),
        in_specs=in_specs,
        out_specs=out_specs,
        out_shape=out_shape,
        input_output_aliases=aliases,
        scratch_shapes=[pltpu.VMEM((tm, k), BF16)],
        compiler_params=_params(("parallel", "arbitrary")),
        name="in_proj",
    )(*args)


def _lambda_value(lq, lam_init):
    a = jnp.sum(lq[0:1] * lq[1:2], axis=-1, keepdims=True)
    b = jnp.sum(lq[2:3] * lq[3:4], axis=-1, keepdims=True)
    return jnp.exp(a) - jnp.exp(b) + lam_init


def _online_softmax_update(s, off, vblk, m_ref, l_ref, acc_ref):
    m_old = m_ref[...]
    m_new = jnp.maximum(m_old, jnp.max(s, axis=-1, keepdims=True) + off)
    p = jnp.exp2(s - (m_new - off))
    alpha = jnp.exp2(m_old - m_new)
    l_ref[...] = alpha * l_ref[...] + jnp.sum(p, axis=-1, keepdims=True)
    acc_ref[...] = alpha * acc_ref[...] + jnp.dot(
        p.astype(BF16), vblk, preferred_element_type=F32)
    m_ref[...] = m_new


_NT_DIMS = (((1,), (1,)), ((), ()))


BIAS_SPLIT = 3
ONES_ROWS = 16
HEADS_PER_STEP = 4


def _bf16_pieces(x):
    out = []
    for _ in range(BIAS_SPLIT):
        hi = x.astype(BF16).astype(F32)
        out.append(hi)
        x = x - hi
    return out


def _lane_select(lane, columns):
    out = jnp.zeros_like(columns[0])
    for i, col in enumerate(columns):
        out = jnp.where(lane == i, col, out)
    return out


def _attn_prompt_kernel(slopes_ref, q_ref, k_ref, v_ref, lq_ref, sg_ref, *rest,
                        tq, lam_init, n_cast):
    src_refs = rest[:n_cast]
    o_ref = rest[n_cast]
    dst_refs = rest[n_cast + 1:2 * n_cast + 1]
    (kb_ref, vt_ref, qe_ref, mask_ref, m_ref, acc_ref, s_ref, pv_ref,
     al_ref) = rest[2 * n_cast + 1:]
    for src, dst in zip(src_refs, dst_refs):
        dst[...] = src[...].astype(BF16)
    hp = pl.program_id(1)
    qi = pl.program_id(2)
    n_blk = vt_ref.shape[1]
    seq = kb_ref.shape[1]
    slopes2 = [slopes_ref[hp * HEADS_PER_STEP + g] * LOG2E for g in range(HEADS_PER_STEP)]

    @pl.when(qi == 0)
    def _():
        lane = lax.broadcasted_iota(jnp.int32, (seq, A_DV), 1)
        kidx = (lax.broadcasted_iota(jnp.int32, (seq, A_DV), 0) & (tq - 1)).astype(F32)
        kside = jnp.where(lane < BIAS_SPLIT, kidx, jnp.where(lane < 2 * BIAS_SPLIT, 1.0, 0.0))
        qlane = lax.broadcasted_iota(jnp.int32, (2 * tq, A_DV), 1)
        qidx = (lax.broadcasted_iota(jnp.int32, (2 * tq, A_DV), 0) & (tq - 1)).astype(F32)
        for g in range(HEADS_PER_STEP):
            cs = slice(g * A_DV, (g + 1) * A_DV)
            kb_ref[g, :, :A_DV] = k_ref[:, cs].astype(BF16)
            kb_ref[g, :, A_DV:] = kside.astype(BF16)
            for jb in range(n_blk):
                vt_ref[g, jb, :A_DV, :] = v_ref[jb * tq:(jb + 1) * tq, cs].T.astype(BF16)
                vt_ref[g, jb, A_DV:, :] = jnp.ones((ONES_ROWS, tq), BF16)
            sl = jnp.full((2 * tq, A_DV), slopes2[g], F32)
            cols = _bf16_pieces(sl) + _bf16_pieces(-(sl * qidx))
            qe_ref[g] = _lane_select(qlane, cols).astype(BF16)
        r = lax.broadcasted_iota(jnp.int32, (tq, 2 * tq), 0)
        c = lax.broadcasted_iota(jnp.int32, (tq, 2 * tq), 1) & (tq - 1)
        mask_ref[...] = jnp.where(r <= c, 0.0, NEG_BIG)

    lane = lax.broadcasted_iota(jnp.int32, (tq, A_DV), 1)
    qs = []
    for g in range(HEADS_PER_STEP):
        q = q_ref[:, g * A_DV:(g + 1) * A_DV] * (A_HD ** -0.5 * LOG2E)
        q2 = jnp.concatenate(
            [jnp.where(lane < A_HD, q, 0.0), jnp.where(lane >= A_HD, q, 0.0)], axis=0)
        qs.append(jnp.concatenate([q2.astype(BF16), qe_ref[g]], axis=1))
    m_ref[...] = jnp.full(m_ref.shape, NEG_BIG, F32)
    acc_ref[...] = jnp.zeros(acc_ref.shape, F32)
    pv_ref[...] = jnp.zeros(pv_ref.shape, F32)
    al_ref[...] = jnp.ones(al_ref.shape, F32)

    def scores(g, j):
        start = pl.multiple_of(j * tq, tq)
        kblk = kb_ref[g, pl.ds(start, tq), :]
        return lax.dot_general(kblk, qs[g], _NT_DIMS, preferred_element_type=F32)

    def softmax_pv(g, j, s, off):
        m_old = m_ref[g]
        m_new = jnp.maximum(m_old, jnp.max(s, axis=0, keepdims=True) + off)
        p = jnp.exp2(s - (m_new - off))
        m_ref[g] = m_new
        pv = jnp.dot(vt_ref[g, j], p.astype(BF16), preferred_element_type=F32)
        return jnp.exp2(m_old - m_new), pv

    for g in range(HEADS_PER_STEP):
        s_ref[0, g] = scores(g, 0)

    def body(j, carry):
        rel = ((j - qi) * tq).astype(F32)
        slot = j & 1
        for g in range(HEADS_PER_STEP):
            s_cur = s_ref[slot, g]
            s_ref[1 - slot, g] = scores(g, j + 1)
            acc_ref[g] = al_ref[g] * acc_ref[g] + pv_ref[g]
            alpha, pv = softmax_pv(g, j, s_cur, slopes2[g] * rel)
            al_ref[g] = alpha
            pv_ref[g] = pv
        return carry

    lax.fori_loop(0, qi, body, 0)
    mask = mask_ref[...]
    lam = _lambda_value(lq_ref[...], lam_init)
    for g in range(HEADS_PER_STEP):
        acc = al_ref[g] * acc_ref[g] + pv_ref[g]
        alpha, pv = softmax_pv(g, qi, s_ref[qi & 1, g] + mask, 0.0)
        acc = alpha * acc + pv
        on = acc[:A_DV] / acc[A_DV:A_DV + 1]
        ot = on[:, :tq] - lam * on[:, tq:]
        yt = ot * lax.rsqrt(jnp.mean(ot * ot, axis=0, keepdims=True) + EPS)
        o_ref[:, g * A_DV:(g + 1) * A_DV] = (
            (yt.T * sg_ref[...]) * (1.0 - lam_init)).astype(BF16)


BF16_SUBLANES = 16


def _cast_rider_specs(cast, n_steps, step_of):
    in_specs, out_specs, out_shape = [], [], []
    for arr, src_layer in cast:
        _, r, c = arr.shape
        rows = pl.cdiv(pl.cdiv(r, n_steps), BF16_SUBLANES) * BF16_SUBLANES
        n_slabs = r // rows
        assert n_slabs * rows == r and n_slabs <= n_steps
        slab = lambda *idx, n_slabs=n_slabs: jnp.minimum(step_of(*idx), n_slabs - 1)
        in_specs.append(pl.BlockSpec(
            (None, rows, c), lambda *idx, slab=slab, sl=src_layer: (sl, slab(*idx), 0)))
        out_specs.append(pl.BlockSpec((None, rows, c), lambda *idx, slab=slab: (0, slab(*idx), 0)))
        out_shape.append(jax.ShapeDtypeStruct((1, r, c), BF16))
    return in_specs, out_specs, out_shape


def attention_prompt(p3, slopes, lam_qk, subln_g3, layer, lam_init, *, tq, cast=()):
    b, s, _ = p3.shape
    gw = HEADS_PER_STEP * A_DV
    n_hg, n_q = A_HEADS // HEADS_PER_STEP, s // tq
    cast_in, cast_out, cast_shape = _cast_rider_specs(
        cast, b * n_hg * n_q, lambda bi, h, qi: (bi * n_hg + h) * n_q + qi)
    kern = functools.partial(_attn_prompt_kernel, tq=tq, lam_init=lam_init, n_cast=len(cast))
    outs = pl.pallas_call(
        kern,
        grid=(b, n_hg, n_q),
        in_specs=[
            pl.BlockSpec(memory_space=pltpu.SMEM),
            pl.BlockSpec((None, tq, gw), lambda bi, h, qi: (bi, qi, COL_Q // gw + h)),
            pl.BlockSpec((None, s, gw), lambda bi, h, qi: (bi, 0, COL_K // gw + h),
                         pipeline_mode=pl.Buffered(1)),
            pl.BlockSpec((None, s, gw), lambda bi, h, qi: (bi, 0, COL_V // gw + h),
                         pipeline_mode=pl.Buffered(1)),
            pl.BlockSpec((None, 4, A_HD), lambda bi, h, qi: (layer, 0, 0)),
            pl.BlockSpec((None, 1, A_DV), lambda bi, h, qi: (layer, 0, 0)),
        ] + cast_in,
        out_specs=[pl.BlockSpec((None, tq, gw), lambda bi, h, qi: (bi, qi, h))] + cast_out,
        out_shape=[jax.ShapeDtypeStruct((b, s, BRANCH_W), BF16)] + cast_shape,
        scratch_shapes=[
            pltpu.VMEM((HEADS_PER_STEP, s, 2 * A_DV), BF16),
            pltpu.VMEM((HEADS_PER_STEP, s // tq, A_DV + ONES_ROWS, tq), BF16),
            pltpu.VMEM((HEADS_PER_STEP, 2 * tq, A_DV), BF16),
            pltpu.VMEM((tq, 2 * tq), F32),
            pltpu.VMEM((HEADS_PER_STEP, 1, 2 * tq), F32),
            pltpu.VMEM((HEADS_PER_STEP, A_DV + ONES_ROWS, 2 * tq), F32),
            pltpu.VMEM((2, HEADS_PER_STEP, tq, 2 * tq), F32),
            pltpu.VMEM((HEADS_PER_STEP, A_DV + ONES_ROWS, 2 * tq), F32),
            pltpu.VMEM((HEADS_PER_STEP, 1, 2 * tq), F32),
        ],
        compiler_params=_params(("parallel", "arbitrary", "arbitrary")),
        name="attn_prompt",
    )(slopes, p3, p3, p3, lam_qk, subln_g3, *[arr for arr, _ in cast])
    return outs[0], outs[1:]


def _attn_sample_kernel(pt_ref, q_ref, kn_ref, vn_ref, slope_ref, lq_ref, sg_ref, *rest,
                        n_pages, n_steps, past_len, lam_init):
    del pt_ref
    kp_refs = rest[:n_pages]
    vp_refs = rest[n_pages:2 * n_pages]
    o_ref, qall_ref, sb_ref, m_ref, l_ref, acc_ref = rest[2 * n_pages:]
    s_idx = pl.program_id(1)
    n_rows = 2 * A_HEADS * SUBLANES
    half = A_HEADS * SUBLANES
    span = n_pages * PAGE_SIZE
    slope = slope_ref[...] * LOG2E

    def head_cols(ref, h):
        return ref[:, h * A_DV:(h + 1) * A_DV]

    @pl.when(s_idx == 0)
    def _():
        lane = lax.broadcasted_iota(jnp.int32, (SUBLANES, A_DV), 1)
        pieces = []
        for mp in range(2):
            keep = (lane >= A_HD) if mp else (lane < A_HD)
            for h in range(A_HEADS):
                pieces.append(
                    jnp.where(keep, head_cols(q_ref, h) * (A_HD ** -0.5 * LOG2E), 0.0))
        qall_ref[...] = jnp.concatenate(pieces, axis=0).astype(BF16)
        r = lax.broadcasted_iota(jnp.int32, sb_ref.shape, 0)
        c = lax.broadcasted_iota(jnp.int32, sb_ref.shape, 1)
        bias = slope * ((c >> 3) - (r & 7)).astype(F32)
        sb_ref[...] = jnp.where((c & 7) == ((r >> 3) & 7), bias, NEG_BIG)
        m_ref[...] = jnp.full(m_ref.shape, NEG_BIG, F32)
        l_ref[...] = jnp.zeros(l_ref.shape, F32)
        acc_ref[...] = jnp.zeros(acc_ref.shape, F32)

    qall = qall_ref[...]
    kcat = jnp.concatenate([r[...].astype(BF16) for r in kp_refs], axis=0)
    vcat = jnp.concatenate([r[...].astype(BF16) for r in vp_refs], axis=0)
    s = lax.dot_general(qall, kcat, _NT_DIMS, preferred_element_type=F32) + sb_ref[...]
    off = slope * (s_idx * span - past_len).astype(F32)
    _online_softmax_update(s, off, vcat, m_ref, l_ref, acc_ref)

    @pl.when(s_idx == n_steps - 1)
    def _():
        pad = jnp.zeros((PAGE_SIZE - half, A_DV), F32)
        kn = jnp.concatenate([head_cols(kn_ref, h) for h in range(A_HEADS)] + [pad], axis=0)
        vn = jnp.concatenate([head_cols(vn_ref, h) for h in range(A_HEADS)] + [pad], axis=0)
        sn = lax.dot_general(qall, kn.astype(BF16), _NT_DIMS, preferred_element_type=F32)
        r = lax.broadcasted_iota(jnp.int32, sn.shape, 0)
        c = lax.broadcasted_iota(jnp.int32, sn.shape, 1)
        qpos = r & 7
        kpos = c & 7
        valid = (c < half) & ((c >> 3) == ((r >> 3) & 7)) & (kpos <= qpos)
        sn = jnp.where(valid, sn + slope * (kpos - qpos).astype(F32), NEG_BIG)
        _online_softmax_update(sn, 0.0, vn.astype(BF16), m_ref, l_ref, acc_ref)

        o_map = acc_ref[...] / l_ref[...]
        lam = _lambda_value(lq_ref[...], lam_init)
        outs = []
        for h in range(A_HEADS):
            o = (o_map[h * SUBLANES:(h + 1) * SUBLANES]
                 - lam * o_map[half + h * SUBLANES:half + (h + 1) * SUBLANES])
            outs.append(_rms(o, sg_ref[...]) * (1.0 - lam_init))
        o_ref[...] = jnp.concatenate(outs, axis=1).astype(BF16)


def attention_sample(p3, cache_k4, cache_v4, page_table, slope_rows, lam_qk, subln_g3,
                     layer, lam_init, *, n_pages):
    b, t, _ = p3.shape
    assert t == SUBLANES
    pages_total = page_table.shape[1]
    n_steps = pages_total // n_pages
    n_rows = 2 * A_HEADS * SUBLANES
    page_rows = PAGE_SIZE * A_HEADS
    kern = functools.partial(
        _attn_sample_kernel, n_pages=n_pages, n_steps=n_steps,
        past_len=pages_total * PAGE_SIZE, lam_init=lam_init)

    def page_spec(i):
        return pl.BlockSpec(
            (None, None, page_rows, A_DV),
            lambda bi, si, pt: (layer, pt[bi, si * n_pages + i], 0, 0))

    col = lambda c: pl.BlockSpec((None, t, BRANCH_W), lambda bi, si, pt: (bi, 0, c // BRANCH_W))
    grid_spec = pltpu.PrefetchScalarGridSpec(
        num_scalar_prefetch=1,
        grid=(b, n_steps),
        in_specs=[
            col(COL_Q), col(COL_K), col(COL_V),
            pl.BlockSpec((n_rows, 1), lambda bi, si, pt: (0, 0)),
            pl.BlockSpec((None, 4, A_HD), lambda bi, si, pt: (layer, 0, 0)),
            pl.BlockSpec((None, 1, A_DV), lambda bi, si, pt: (layer, 0, 0)),
        ] + [page_spec(i) for i in range(n_pages)] * 2,
        out_specs=pl.BlockSpec((None, t, BRANCH_W), lambda bi, si, pt: (bi, 0, 0)),
        scratch_shapes=[
            pltpu.VMEM((n_rows, A_DV), BF16),
            pltpu.VMEM((n_rows, n_pages * page_rows), F32),
            pltpu.VMEM((n_rows, 1), F32),
            pltpu.VMEM((n_rows, 1), F32),
            pltpu.VMEM((n_rows, A_DV), F32),
        ],
    )
    return pl.pallas_call(
        kern,
        grid_spec=grid_spec,
        out_shape=jax.ShapeDtypeStruct((b, t, BRANCH_W), BF16),
        compiler_params=_params(("parallel", "arbitrary")),
        name="attn_sample",
    )(page_table, p3, p3, p3, slope_rows, lam_qk, subln_g3,
      *([cache_k4] * n_pages), *([cache_v4] * n_pages))


def _shift_rows(x, prev, j, row8):
    xr = pltpu.roll(x, j, 0)
    head = jnp.where(row8 < j, pltpu.roll(prev, j, 0), xr[:SUBLANES])
    if x.shape[0] == SUBLANES:
        return head
    return jnp.concatenate([head, xr[SUBLANES:]], axis=0)


def _lru_kernel(x_ref, c0_ref, h0_ref, cw_ref, cb_ref, wr_ref, wi_ref, br_ref, bi_ref, lam_ref,
                hs_ref, hl_ref, co_ref, hprev_ref, xprev_ref):
    t_idx = pl.program_id(2)

    @pl.when(t_idx == 0)
    def _():
        hprev_ref[...] = h0_ref[...]
        xprev_ref[...] = c0_ref[...]

    x = x_ref[...]
    n_t, tc = x.shape
    prev = xprev_ref[...]
    row8 = lax.broadcasted_iota(jnp.int32, (SUBLANES, tc), 0)
    cw = cw_ref[...]
    xc = cb_ref[...] + cw[0:1] * _shift_rows(x, prev, 3, row8)
    xc = xc + cw[1:2] * _shift_rows(x, prev, 2, row8)
    xc = xc + cw[2:3] * _shift_rows(x, prev, 1, row8)
    xc = xc + cw[3:4] * x
    tail = x[n_t - SUBLANES:]
    xprev_ref[...] = tail
    co_ref[...] = tail

    xcb = xc.astype(BF16)
    r = jax.nn.sigmoid(jnp.dot(xcb, wr_ref[...], preferred_element_type=F32) + br_ref[...])
    i = jax.nn.sigmoid(jnp.dot(xcb, wi_ref[...], preferred_element_type=F32) + bi_ref[...])
    nl = -lam_ref[...]
    softplus = jnp.maximum(nl, 0.0) + jnp.log1p(jnp.exp(-jnp.abs(nl)))
    log_a = (-LRU_C) * r * softplus
    a = jnp.exp(log_a)
    th = jnp.tanh(log_a)
    u = jnp.sqrt(-2.0 * th / (1.0 - th)) * (i * xc)

    n_g = n_t // SUBLANES
    a = a.reshape(n_g, SUBLANES, tc)
    u = u.reshape(n_g, SUBLANES, tc)
    row = lax.broadcasted_iota(jnp.int32, a.shape, 1)
    d = 1
    while d < SUBLANES:
        a_sh = jnp.where(row >= d, pltpu.roll(a, d, 1), 1.0)
        u_sh = jnp.where(row >= d, pltpu.roll(u, d, 1), 0.0)
        u = u + a * u_sh
        a = a * a_sh
        d *= 2
    carry = hprev_ref[...]
    groups = []
    for g in range(n_g):
        groups.append(u[g] + a[g] * carry)
        carry = groups[-1][SUBLANES - 1:]
    hs_ref[...] = jnp.concatenate(groups, axis=0).astype(BF16)
    hprev_ref[...] = carry
    hl_ref[...] = carry


def rg_lru(p3, conv0, h0, conv_w, conv_b3, wr_bd, wi_bd, b_rg3, b_ig3, lam3, layer, *, tt, tc):
    b, s, _ = p3.shape
    c = LRU_W
    vec = lambda: pl.BlockSpec((None, 1, tc), lambda bi, ci, ti: (layer, 0, ci))
    gate_w = lambda: pl.BlockSpec((None, None, tc, tc), lambda bi, ci, ti: (layer, ci, 0, 0))
    state = lambda rows: pl.BlockSpec((None, rows, tc), lambda bi, ci, ti: (bi, 0, ci))
    return pl.pallas_call(
        _lru_kernel,
        grid=(b, c // tc, s // tt),
        in_specs=[
            pl.BlockSpec((None, tt, tc), lambda bi, ci, ti: (bi, ti, COL_XB // tc + ci)),
            state(SUBLANES), state(1),
            pl.BlockSpec((None, CONV_W, tc), lambda bi, ci, ti: (layer, 0, ci)),
            vec(), gate_w(), gate_w(), vec(), vec(), vec(),
        ],
        out_specs=[
            pl.BlockSpec((None, tt, tc), lambda bi, ci, ti: (bi, ti, ci)),
            state(1), state(SUBLANES),
        ],
        out_shape=[
            jax.ShapeDtypeStruct((b, s, c), BF16),
            jax.ShapeDtypeStruct((b, 1, c), F32),
            jax.ShapeDtypeStruct((b, SUBLANES, c), F32),
        ],
        scratch_shapes=[pltpu.VMEM((1, tc), F32), pltpu.VMEM((SUBLANES, tc), F32)],
        compiler_params=_params(("parallel", "parallel", "arbitrary")),
        name="rg_lru",
    )(p3, conv0, h0, conv_w, conv_b3, wr_bd, wi_bd, b_rg3, b_ig3, lam3)


def _gelu_tanh(x):
    c = math.sqrt(2.0 / math.pi)
    return 0.5 * x * (1.0 + jnp.tanh(c * (x + 0.044715 * (x * x * x))))


def _cmlp_kernel(zu_ref, zv_ref, lg_ref, lb_ref, ws_ref, bst_ref, oc_ref, *maybe_vc_ref, ch):
    u = _gelu_tanh(zu_ref[...])
    vg = _gelu_tanh(zv_ref[...])
    xc = vg - jnp.mean(vg, axis=-1, keepdims=True)
    v = xc * lax.rsqrt(jnp.mean(xc * xc, axis=-1, keepdims=True) + EPS) * lg_ref[...] + lb_ref[...]
    if maybe_vc_ref:
        maybe_vc_ref[0][...] = v
    n_ch = u.shape[0] // ch
    r = lax.broadcasted_iota(jnp.int32, (ch, ch), 0)
    c = lax.broadcasted_iota(jnp.int32, (ch, ch), 1)
    for g in range(C_GROUPS):
        wm = jnp.where(c <= r, ws_ref[g][:ch, :ch], 0.0)
        bias = bst_ref[:ch, g:g + 1]
        cs = slice(g * C_GW, (g + 1) * C_GW)
        for n in range(n_ch):
            rs = slice(n * ch, (n + 1) * ch)
            vb = v[rs, cs]
            if ch >= LANES:
                f = jnp.dot(wm.astype(BF16), vb.astype(BF16), preferred_element_type=F32) + bias
            else:
                f = bias + wm[:, 0:1] * vb[0:1, :]
                for s in range(1, ch):
                    f = f + wm[:, s:s + 1] * vb[s:s + 1, :]
            oc_ref[rs, cs] = (u[rs, cs] * f).astype(BF16)


def chunk_mlp(p, ln_g3, ln_b3, w_s, b_st, layer, *, tm, ch, emit_v):
    m = p.shape[0]
    vec = lambda: pl.BlockSpec((None, 1, C_W), lambda i: (layer, 0, 0))
    out_specs = [pl.BlockSpec((tm, C_W), lambda i: (i, 0))]
    out_shape = [jax.ShapeDtypeStruct((m, C_W), BF16)]
    if emit_v:
        out_specs.append(pl.BlockSpec((tm, C_W), lambda i: (i, 0)))
        out_shape.append(jax.ShapeDtypeStruct((m, C_W), F32))
    return pl.pallas_call(
        functools.partial(_cmlp_kernel, ch=ch),
        grid=(m // tm,),
        in_specs=[
            pl.BlockSpec((tm, C_W), lambda i: (i, COL_Z // C_W)),
            pl.BlockSpec((tm, C_W), lambda i: (i, COL_Z // C_W + 1)),
            vec(), vec(),
            pl.BlockSpec((None, C_GROUPS, CHUNK, CHUNK), lambda i: (layer, 0, 0, 0)),
            pl.BlockSpec((None, CHUNK, C_GROUPS), lambda i: (layer, 0, 0)),
        ],
        out_specs=out_specs,
        out_shape=out_shape,
        compiler_params=_params(("parallel",)),
        name="chunk_mlp",
    )(p, p, ln_g3, ln_b3, w_s, b_st)


def _branch_kernel(oa_ref, ob_ref, oc_ref, g0_ref, g1_ref, g2_ref, wb_ref, m_ref):
    acc = None
    for j, (o_ref, g_ref) in enumerate(((oa_ref, g0_ref), (ob_ref, g1_ref), (oc_ref, g2_ref))):
        y = jnp.dot(o_ref[...], wb_ref[j], preferred_element_type=F32)
        t = g_ref[...].astype(F32) * y
        acc = t if acc is None else acc + t
    m_ref[...] = acc.astype(BF16)


def branch_merge(o_a, o_b, o_c, gates, wb, *, tm, tn):
    m = o_a.shape[0]
    o_spec = lambda: pl.BlockSpec((tm, BRANCH_W), lambda i, j: (i, 0))
    g_spec = lambda b: pl.BlockSpec((tm, tn), lambda i, j: (i, b * D_MODEL // tn + j))
    return pl.pallas_call(
        _branch_kernel,
        grid=(m // tm, D_MODEL // tn),
        in_specs=[o_spec(), o_spec(), o_spec(), g_spec(0), g_spec(1), g_spec(2),
                  pl.BlockSpec((None, N_BRANCH, BRANCH_W, tn), lambda i, j: (0, 0, 0, j))],
        out_specs=pl.BlockSpec((tm, tn), lambda i, j: (i, j)),
        out_shape=jax.ShapeDtypeStruct((m, D_MODEL), BF16),
        compiler_params=_params(("parallel", "arbitrary")),
        name="branch_merge",
    )(o_a, o_b, o_c, gates, gates, gates, wb)


def _mm_res_kernel(x_ref, w_ref, h_ref, o_ref):
    o_ref[...] = h_ref[...] + jnp.dot(x_ref[...], w_ref[...], preferred_element_type=F32)


def matmul_residual(x, w3, h, *, tm, tn):
    m, k = x.shape
    n = w3.shape[-1]
    return pl.pallas_call(
        _mm_res_kernel,
        grid=(m // tm, n // tn),
        in_specs=[
            pl.BlockSpec((tm, k), lambda i, j: (i, 0)),
            pl.BlockSpec((None, k, tn), lambda i, j: (0, 0, j)),
            pl.BlockSpec((tm, tn), lambda i, j: (i, j)),
        ],
        out_specs=pl.BlockSpec((tm, tn), lambda i, j: (i, j)),
        out_shape=jax.ShapeDtypeStruct((m, n), F32),
        compiler_params=_params(("parallel", "arbitrary")),
        name="out_proj",
    )(x, w3, h)


def _ffn_kernel(h_ref, g_ref, w1_ref, w2_ref, o_ref, xn_ref):
    @pl.when(pl.program_id(1) == 0)
    def _():
        h = h_ref[...]
        xn_ref[...] = _rms(h, g_ref[...]).astype(BF16)
        o_ref[...] = h

    hid = jnp.maximum(jnp.dot(xn_ref[...], w1_ref[...], preferred_element_type=F32), 0.0)
    o_ref[...] += jnp.dot((hid * hid).astype(BF16), w2_ref[...], preferred_element_type=F32)


def ffn(h, g3, w1, w2, layer, *, tm, tf):
    m, d = h.shape
    return pl.pallas_call(
        _ffn_kernel,
        grid=(m // tm, FF_W // tf),
        in_specs=[
            pl.BlockSpec((tm, d), lambda i, j: (i, 0)),
            pl.BlockSpec((None, 1, d), lambda i, j: (layer, 0, 0)),
            pl.BlockSpec((None, d, tf), lambda i, j: (0, 0, j)),
            pl.BlockSpec((None, tf, d), lambda i, j: (0, j, 0)),
        ],
        out_specs=pl.BlockSpec((tm, d), lambda i, j: (i, 0)),
        out_shape=jax.ShapeDtypeStruct((m, d), F32),
        scratch_shapes=[pltpu.VMEM((tm, d), BF16)],
        compiler_params=_params(("parallel", "arbitrary")),
        name="ffn",
    )(h, g3, w1, w2)


def _final_norm_kernel(x_ref, g_ref, o_ref):
    o_ref[...] = _rms(x_ref[...], g_ref[...])


def final_norm(x, g2, *, tm):
    m, d = x.shape
    return pl.pallas_call(
        _final_norm_kernel,
        grid=(m // tm,),
        in_specs=[pl.BlockSpec((tm, d), lambda i: (i, 0)), pl.BlockSpec((1, d), lambda i: (0, 0))],
        out_specs=pl.BlockSpec((tm, d), lambda i: (i, 0)),
        out_shape=jax.ShapeDtypeStruct((m, d), F32),
        compiler_params=_params(("parallel",)),
        name="final_norm",
    )(x, g2)


def _block_diag_gates(w):
    per = 256 // LRU_BW
    w5 = w.reshape(DEPTH, LRU_BLOCKS // per, per, LRU_BW, LRU_BW)
    eye = jnp.eye(per, dtype=w.dtype)
    bd = jnp.einsum("lgbij,bc->lgbicj", w5, eye)
    return bd.reshape(DEPTH, LRU_BLOCKS // per, 256, 256).astype(BF16)


def _trunk_layer(h, layer, wts, w_in_l, lw, attend, conv0, h0, *, tiles, ch, emit_v,
                 kv_stack=None):
    bsz, seq, _ = h.shape
    m = bsz * seq
    h2 = h.reshape(m, D_MODEL)
    if kv_stack is None:
        p, gates = norm_matmul(
            h2, wts["norm1_g"], w_in_l, layer, tm=tiles["tm"], tn=tiles["tn_in"])
    else:
        p, gates, *kv_stack = norm_matmul(
            h2, wts["norm1_g"], w_in_l, layer, tm=tiles["tm"], tn=tiles["tn_in"],
            caches=tuple(kv_stack) or None, emit_cache=True)
    p3 = p.reshape(bsz, seq, COL_G)
    o_a, copies = attend(p3)
    o_a = o_a.reshape(m, BRANCH_W)
    next_w_in = None
    if lw is None:
        lw = dict(zip(("w_branch", "w_out", "w_ff1", "w_ff2"), copies))
        lw["w_branch"] = lw["w_branch"].reshape(1, N_BRANCH, BRANCH_W, D_MODEL)
        next_w_in = copies[4] if len(copies) > 4 else None
    o_b, h_last, conv_tail = rg_lru(
        p3, conv0, h0, wts["conv_w"], wts["conv_b"], wts["w_rg"], wts["w_ig"],
        wts["b_rg"], wts["b_ig"], wts["lru_lambda"], layer, tt=tiles["tt"], tc=256)
    cm = chunk_mlp(p, wts["cmlp_ln_g"], wts["cmlp_ln_b"], wts["w_s"], wts["b_st"], layer,
                   tm=tiles["tm_c"], ch=ch, emit_v=emit_v)
    mrg = branch_merge(o_a, o_b.reshape(m, BRANCH_W), cm[0], gates, lw["w_branch"],
                       tm=tiles["tm"], tn=tiles["tn_b"])
    h2 = matmul_residual(mrg, lw["w_out"], h2, tm=tiles["tm"], tn=tiles["tn"])
    h2 = ffn(h2, wts["norm2_g"], lw["w_ff1"], lw["w_ff2"], layer,
             tm=tiles["tm_f"], tf=tiles["tf"])
    if kv_stack is None:
        k = p3[..., COL_K:COL_V].reshape(bsz, seq, A_HEADS, 2 * A_HD)
        v = p3[..., COL_V:COL_XB].reshape(bsz, seq, A_HEADS, A_DV)
    else:
        k, v = kv_stack
    new_buf = conv_tail[:, SUBLANES - (CONV_W - 1):]
    v_c = cm[1].reshape(bsz, seq, C_W) if emit_v else None
    return h2.reshape(bsz, seq, D_MODEL), k, v, new_buf, h_last[:, 0], v_c, lw, next_w_in


def kernel(x_prompt, x_sample, cache_k, cache_v, state_h, state_conv, page_table,
           norm1_g, w_in, lam_qk, subln_g, conv_w, conv_b, w_rg, b_rg, w_ig, b_ig, lru_lambda,
           cmlp_ln_g, cmlp_ln_b, w_s, b_s, w_branch, w_out, norm2_g, w_ff1, w_ff2, final_g):
    bp, sp, _ = x_prompt.shape
    bs, ss, _ = x_sample.shape
    n_pool = cache_k.shape[1]
    row = lambda a: a.reshape(DEPTH, 1, a.shape[-1])
    wts = {
        "norm1_g": row(norm1_g),
        "conv_w": conv_w, "conv_b": row(conv_b),
        "w_rg": _block_diag_gates(w_rg), "w_ig": _block_diag_gates(w_ig),
        "b_rg": row(b_rg), "b_ig": row(b_ig), "lru_lambda": row(lru_lambda),
        "cmlp_ln_g": row(cmlp_ln_g), "cmlp_ln_b": row(cmlp_ln_b),
        "w_s": w_s, "b_st": jnp.swapaxes(b_s, 1, 2),
        "norm2_g": row(norm2_g),
    }
    late_views = (w_branch.reshape(DEPTH, N_BRANCH * BRANCH_W, D_MODEL), w_out, w_ff1, w_ff2)
    w_in_l = w_in[0:1].astype(BF16)
    subln_g3 = row(subln_g)
    slopes = jnp.exp2(-8.0 * jnp.arange(1, A_HEADS + 1, dtype=F32) / A_HEADS)
    slope_rows = jnp.tile(jnp.repeat(slopes, SUBLANES), 2).reshape(2 * A_HEADS * SUBLANES, 1)
    cache_k4 = cache_k.reshape(DEPTH, n_pool, PAGE_SIZE * A_HEADS, A_DV)
    cache_v4 = cache_v.reshape(DEPTH, n_pool, PAGE_SIZE * A_HEADS, A_DV)
    conv0_p = jnp.zeros((bp, SUBLANES, LRU_W), F32)
    h0_p = jnp.zeros((bp, 1, LRU_W), F32)
    conv0_s = jnp.pad(state_conv, ((0, 0), (0, 0), (SUBLANES - (CONV_W - 1), 0), (0, 0)))

    tiles_p = dict(tm=1024, tn_in=1024, tn_b=512, tn=1024, tt=256, tm_c=512, tm_f=512, tf=1024)
    ms = bs * ss
    tiles_s = dict(tm=ms, tn_in=512, tn_b=512, tn=512, tt=ss, tm_c=ms, tm_f=ms, tf=1024)

    hp, hs = x_prompt, x_sample
    outs = [[] for _ in range(7)]
    kv_stack = ()
    for l in range(DEPTH):
        lam_init = 0.8 - 0.6 * math.exp(-0.3 * l)
        cast = tuple((v, l) for v in late_views) + (((w_in, l + 1),) if l + 1 < DEPTH else ())
        attend_p = functools.partial(
            attention_prompt, slopes=slopes, lam_qk=lam_qk, subln_g3=subln_g3,
            layer=l, lam_init=lam_init, tq=256, cast=cast)
        hp, kp_stack, vp_stack, cp, hlp, _, lw, next_w_in = _trunk_layer(
            hp, l, wts, w_in_l, None, attend_p, conv0_p, h0_p, tiles=tiles_p, ch=CHUNK,
            emit_v=False, kv_stack=kv_stack)
        kv_stack = (kp_stack, vp_stack)

        def attend_s(p3, l=l, lam_init=lam_init):
            return attention_sample(
                p3, cache_k4, cache_v4, page_table, slope_rows, lam_qk, subln_g3,
                l, lam_init, n_pages=8), ()

        hs, k_s, v_s, c_s, h_s, vc_s, _, _ = _trunk_layer(
            hs, l, wts, w_in_l, lw, attend_s, conv0_s[l], state_h[l][:, None, :],
            tiles=tiles_s, ch=min(ss, CHUNK), emit_v=True)
        w_in_l = next_w_in
        for lst, val in zip(outs, (hlp, cp, k_s, v_s, h_s, c_s, vc_s)):
            lst.append(val)
    y_prompt = final_norm(hp.reshape(bp * sp, D_MODEL), final_g.reshape(1, D_MODEL), tm=1024)
    y_sample = final_norm(hs.reshape(ms, D_MODEL), final_g.reshape(1, D_MODEL), tm=ms)
    stacked = [jnp.stack(lst) for lst in outs]
    k_prompt, v_prompt = (a.reshape(DEPTH, bp, sp, A_HEADS, A_DV) for a in kv_stack)
    return (y_prompt.reshape(bp, sp, D_MODEL), y_sample.reshape(bs, ss, D_MODEL),
            k_prompt, v_prompt, *stacked)
```

```python
import functools
import math

import jax
import jax.numpy as jnp
from jax import lax
from jax.experimental import pallas as pl
from jax.experimental.pallas import tpu as pltpu

F32 = jnp.float32
BF16 = jnp.bfloat16

D_MODEL = 2048
DEPTH = 4
PAGE_SIZE = 128
BRANCH_W = D_MODEL // 2
A_HD = 64
A_HEADS = BRANCH_W // (2 * A_HD)
A_DV = 2 * A_HD
LRU_W = BRANCH_W
LRU_BLOCKS = 16
LRU_BW = LRU_W // LRU_BLOCKS
CONV_W = 4
LRU_C = 8.0
C_W = BRANCH_W
C_GROUPS = 4
C_GW = C_W // C_GROUPS
CHUNK = 128
N_BRANCH = 3
FF_W = 4 * D_MODEL
EPS = 1e-6

COL_Q = 0
COL_K = BRANCH_W
COL_V = 2 * BRANCH_W
COL_XB = 3 * BRANCH_W
COL_Z = 4 * BRANCH_W
COL_G = 6 * BRANCH_W
W_IN = COL_G + N_BRANCH * D_MODEL

LANES = 128
SUBLANES = 8
NEG_BIG = -1e30
LOG2E = math.log2(math.e)
MIB = 1024 * 1024
VMEM_LIMIT = 52 * MIB


def _params(semantics):
    return pltpu.CompilerParams(dimension_semantics=semantics, vmem_limit_bytes=VMEM_LIMIT)


def _rms(x, g):
    return x * lax.rsqrt(jnp.mean(x * x, axis=-1, keepdims=True) + EPS) * g


def _norm_mm_kernel(x_ref, g_ref, w_ref, *rest, tn, n_passthrough):
    rest = rest[n_passthrough:]
    o_ref, gate_ref, xn_ref = rest[0], rest[1], rest[-1]
    cache_refs = rest[2:-1]
    j = pl.program_id(1)
    n_main = COL_G // tn

    @pl.when(j == 0)
    def _():
        xn_ref[...] = _rms(x_ref[...], g_ref[...]).astype(BF16)

    @pl.when(j < n_main)
    def _():
        o_ref[...] = jnp.dot(xn_ref[...], w_ref[...], preferred_element_type=F32)

    @pl.when(j >= n_main)
    def _():
        pre = jnp.dot(xn_ref[...], w_ref[...], preferred_element_type=F32)
        gate_ref[...] = (0.5 * (jnp.tanh(0.5 * pre) + 1.0)).astype(BF16)

    tm = o_ref.shape[0]
    heads_per_tile = tn // A_DV
    for dst, col0 in zip(cache_refs, (COL_K, COL_V)):
        for t in range(BRANCH_W // tn):
            @pl.when(j == col0 // tn + t)
            def _(dst=dst, t=t):
                for hh in range(heads_per_tile):
                    h = t * heads_per_tile + hh
                    dst[pl.ds(h, tm, stride=A_HEADS), :] = o_ref[:, hh * A_DV:(hh + 1) * A_DV]


def norm_matmul(x, g3, w3, layer, *, tm, tn, caches=None, emit_cache=False):
    m, k = x.shape
    n = w3.shape[-1]
    n_main = COL_G // tn
    in_specs = [
        pl.BlockSpec((tm, k), lambda i, j: (i, 0), pipeline_mode=pl.Buffered(1)),
        pl.BlockSpec((None, 1, k), lambda i, j: (layer, 0, 0)),
        pl.BlockSpec((None, k, tn), lambda i, j: (0, 0, j)),
    ]
    out_specs = [
        pl.BlockSpec((tm, tn), lambda i, j: (i, jnp.minimum(j, n_main - 1))),
        pl.BlockSpec((tm, tn), lambda i, j: (i, jnp.maximum(j - n_main, 0))),
    ]
    out_shape = [jax.ShapeDtypeStruct((m, COL_G), F32),
                 jax.ShapeDtypeStruct((m, n - COL_G), BF16)]
    args = [x, g3, w3]
    aliases = {}
    if emit_cache:
        for c in range(2):
            out_specs.append(pl.BlockSpec((None, tm * A_HEADS, A_DV), lambda i, j: (layer, i, 0),
                                          pipeline_mode=pl.Buffered(1)))
            out_shape.append(jax.ShapeDtypeStruct((DEPTH, m * A_HEADS, A_DV), F32))
            if caches is not None:
                in_specs.append(pl.BlockSpec(memory_space=pl.ANY))
                aliases[len(args)] = 2 + c
                args.append(caches[c])
    kern = functools.partial(_norm_mm_kernel, tn=tn, n_passthrough=len(aliases))
    return pl.pallas_call(
        kern,
        grid=(m // tm, n // tn),
        in_specs=in_specs,
        out_specs=out_specs,
        out_shape=out_shape,
        input_output_aliases=aliases,
        scratch_shapes=[pltpu.VMEM((tm, k), BF16)],
        compiler_params=_params(("parallel", "arbitrary")),
        name="in_proj",
    )(*args)


def _lambda_value(lq, lam_init):
    a = jnp.sum(lq[0:1] * lq[1:2], axis=-1, keepdims=True)
    b = jnp.sum(lq[2:3] * lq[3:4], axis=-1, keepdims=True)
    return jnp.exp(a) - jnp.exp(b) + lam_init


def _online_softmax_update(s, off, vblk, m_ref, l_ref, acc_ref):
    m_old = m_ref[...]
    m_new = jnp.maximum(m_old, jnp.max(s, axis=-1, keepdims=True) + off)
    p = jnp.exp2(s - (m_new - off))
    alpha = jnp.exp2(m_old - m_new)
    l_ref[...] = alpha * l_ref[...] + jnp.sum(p, axis=-1, keepdims=True)
    acc_ref[...] = alpha * acc_ref[...] + jnp.dot(
        p.astype(BF16), vblk, preferred_element_type=F32)
    m_ref[...] = m_new


_NT_DIMS = (((1,), (1,)), ((), ()))


BIAS_SPLIT = 3
ONES_ROWS = 16
HEADS_PER_STEP = 4


def _bf16_pieces(x):
    out = []
    for _ in range(BIAS_SPLIT):
        hi = x.astype(BF16).astype(F32)
        out.append(hi)
        x = x - hi
    return out


def _lane_select(lane, columns):
    out = jnp.zeros_like(columns[0])
    for i, col in enumerate(columns):
        out = jnp.where(lane == i, col, out)
    return out


def _attn_prompt_kernel(slopes_ref, q_ref, k_ref, v_ref, lq_ref, sg_ref, *rest,
                        tq, lam_init, n_cast):
    src_refs = rest[:n_cast]
    o_ref = rest[n_cast]
    dst_refs = rest[n_cast + 1:2 * n_cast + 1]
    (kb_ref, vt_ref, qe_ref, mask_ref, m_ref, acc_ref, s_ref, pv_ref,
     al_ref) = rest[2 * n_cast + 1:]
    for src, dst in zip(src_refs, dst_refs):
        dst[...] = src[...].astype(BF16)
    hp = pl.program_id(1)
    qi = pl.program_id(2)
    n_blk = vt_ref.shape[1]
    seq = kb_ref.shape[1]
    slopes2 = [slopes_ref[hp * HEADS_PER_STEP + g] * LOG2E for g in range(HEADS_PER_STEP)]

    @pl.when(qi == 0)
    def _():
        lane = lax.broadcasted_iota(jnp.int32, (seq, A_DV), 1)
        kidx = (lax.broadcasted_iota(jnp.int32, (seq, A_DV), 0) & (tq - 1)).astype(F32)
        kside = jnp.where(lane < BIAS_SPLIT, kidx, jnp.where(lane < 2 * BIAS_SPLIT, 1.0, 0.0))
        qlane = lax.broadcasted_iota(jnp.int32, (2 * tq, A_DV), 1)
        qidx = (lax.broadcasted_iota(jnp.int32, (2 * tq, A_DV), 0) & (tq - 1)).astype(F32)
        for g in range(HEADS_PER_STEP):
            cs = slice(g * A_DV, (g + 1) * A_DV)
            kb_ref[g, :, :A_DV] = k_ref[:, cs].astype(BF16)
            kb_ref[g, :, A_DV:] = kside.astype(BF16)
            for jb in range(n_blk):
                vt_ref[g, jb, :A_DV, :] = v_ref[jb * tq:(jb + 1) * tq, cs].T.astype(BF16)
                vt_ref[g, jb, A_DV:, :] = jnp.ones((ONES_ROWS, tq), BF16)
            sl = jnp.full((2 * tq, A_DV), slopes2[g], F32)
            cols = _bf16_pieces(sl) + _bf16_pieces(-(sl * qidx))
            qe_ref[g] = _lane_select(qlane, cols).astype(BF16)
        r = lax.broadcasted_iota(jnp.int32, (tq, 2 * tq), 0)
        c = lax.broadcasted_iota(jnp.int32, (tq, 2 * tq), 1) & (tq - 1)
        mask_ref[...] = jnp.where(r <= c, 0.0, NEG_BIG)

    lane = lax.broadcasted_iota(jnp.int32, (tq, A_DV), 1)
    qs = []
    for g in range(HEADS_PER_STEP):
        q = q_ref[:, g * A_DV:(g + 1) * A_DV] * (A_HD ** -0.5 * LOG2E)
        q2 = jnp.concatenate(
            [jnp.where(lane < A_HD, q, 0.0), jnp.where(lane >= A_HD, q, 0.0)], axis=0)
        qs.append(jnp.concatenate([q2.astype(BF16), qe_ref[g]], axis=1))
    m_ref[...] = jnp.full(m_ref.shape, NEG_BIG, F32)
    acc_ref[...] = jnp.zeros(acc_ref.shape, F32)
    pv_ref[...] = jnp.zeros(pv_ref.shape, F32)
    al_ref[...] = jnp.ones(al_ref.shape, F32)

    def scores(g, j):
        start = pl.multiple_of(j * tq, tq)
        kblk = kb_ref[g, pl.ds(start, tq), :]
        return lax.dot_general(kblk, qs[g], _NT_DIMS, preferred_element_type=F32)

    def softmax_pv(g, j, s, off):
        m_old = m_ref[g]
        m_new = jnp.maximum(m_old, jnp.max(s, axis=0, keepdims=True) + off)
        p = jnp.exp2(s - (m_new - off))
        m_ref[g] = m_new
        pv = jnp.dot(vt_ref[g, j], p.astype(BF16), preferred_element_type=F32)
        return jnp.exp2(m_old - m_new), pv

    for g in range(HEADS_PER_STEP):
        s_ref[0, g] = scores(g, 0)

    def body(j, carry):
        rel = ((j - qi) * tq).astype(F32)
        slot = j & 1
        for g in range(HEADS_PER_STEP):
            s_cur = s_ref[slot, g]
            s_ref[1 - slot, g] = scores(g, j + 1)
            acc_ref[g] = al_ref[g] * acc_ref[g] + pv_ref[g]
            alpha, pv = softmax_pv(g, j, s_cur, slopes2[g] * rel)
            al_ref[g] = alpha
            pv_ref[g] = pv
        return carry

    lax.fori_loop(0, qi, body, 0)
    mask = mask_ref[...]
    lam = _lambda_value(lq_ref[...], lam_init)
    for g in range(HEADS_PER_STEP):
        acc = al_ref[g] * acc_ref[g] + pv_ref[g]
        alpha, pv = softmax_pv(g, qi, s_ref[qi & 1, g] + mask, 0.0)
        acc = alpha * acc + pv
        on = acc[:A_DV] / acc[A_DV:A_DV + 1]
        ot = on[:, :tq] - lam * on[:, tq:]
        yt = ot * lax.rsqrt(jnp.mean(ot * ot, axis=0, keepdims=True) + EPS)
        o_ref[:, g * A_DV:(g + 1) * A_DV] = (
            (yt.T * sg_ref[...]) * (1.0 - lam_init)).astype(BF16)


BF16_SUBLANES = 16


def _cast_rider_specs(cast, n_steps, step_of):
    in_specs, out_specs, out_shape = [], [], []
    for arr, src_layer in cast:
        _, r, c = arr.shape
        rows = pl.cdiv(pl.cdiv(r, n_steps), BF16_SUBLANES) * BF16_SUBLANES
        n_slabs = r // rows
        assert n_slabs * rows == r and n_slabs <= n_steps
        slab = lambda *idx, n_slabs=n_slabs: jnp.minimum(step_of(*idx), n_slabs - 1)
        in_specs.append(pl.BlockSpec(
            (None, rows, c), lambda *idx, slab=slab, sl=src_layer: (sl, slab(*idx), 0)))
        out_specs.append(pl.BlockSpec((None, rows, c), lambda *idx, slab=slab: (0, slab(*idx), 0)))
        out_shape.append(jax.ShapeDtypeStruct((1, r, c), BF16))
    return in_specs, out_specs, out_shape


def attention_prompt(p3, slopes, lam_qk, subln_g3, layer, lam_init, *, tq, cast=()):
    b, s, _ = p3.shape
    gw = HEADS_PER_STEP * A_DV
    n_hg, n_q = A_HEADS // HEADS_PER_STEP, s // tq
    cast_in, cast_out, cast_shape = _cast_rider_specs(
        cast, b * n_hg * n_q, lambda bi, h, qi: (bi * n_hg + h) * n_q + qi)
    kern = functools.partial(_attn_prompt_kernel, tq=tq, lam_init=lam_init, n_cast=len(cast))
    outs = pl.pallas_call(
        kern,
        grid=(b, n_hg, n_q),
        in_specs=[
            pl.BlockSpec(memory_space=pltpu.SMEM),
            pl.BlockSpec((None, tq, gw), lambda bi, h, qi: (bi, qi, COL_Q // gw + h)),
            pl.BlockSpec((None, s, gw), lambda bi, h, qi: (bi, 0, COL_K // gw + h),
                         pipeline_mode=pl.Buffered(1)),
            pl.BlockSpec((None, s, gw), lambda bi, h, qi: (bi, 0, COL_V // gw + h),
                         pipeline_mode=pl.Buffered(1)),
            pl.BlockSpec((None, 4, A_HD), lambda bi, h, qi: (layer, 0, 0)),
            pl.BlockSpec((None, 1, A_DV), lambda bi, h, qi: (layer, 0, 0)),
        ] + cast_in,
        out_specs=[pl.BlockSpec((None, tq, gw), lambda bi, h, qi: (bi, qi, h))] + cast_out,
        out_shape=[jax.ShapeDtypeStruct((b, s, BRANCH_W), BF16)] + cast_shape,
        scratch_shapes=[
            pltpu.VMEM((HEADS_PER_STEP, s, 2 * A_DV), BF16),
            pltpu.VMEM((HEADS_PER_STEP, s // tq, A_DV + ONES_ROWS, tq), BF16),
            pltpu.VMEM((HEADS_PER_STEP, 2 * tq, A_DV), BF16),
            pltpu.VMEM((tq, 2 * tq), F32),
            pltpu.VMEM((HEADS_PER_STEP, 1, 2 * tq), F32),
            pltpu.VMEM((HEADS_PER_STEP, A_DV + ONES_ROWS, 2 * tq), F32),
            pltpu.VMEM((2, HEADS_PER_STEP, tq, 2 * tq), F32),
            pltpu.VMEM((HEADS_PER_STEP, A_DV + ONES_ROWS, 2 * tq), F32),
            pltpu.VMEM((HEADS_PER_STEP, 1, 2 * tq), F32),
        ],
        compiler_params=_params(("parallel", "arbitrary", "arbitrary")),
        name="attn_prompt",
    )(slopes, p3, p3, p3, lam_qk, subln_g3, *[arr for arr, _ in cast])
    return outs[0], outs[1:]


def _attn_sample_kernel(pt_ref, q_ref, kn_ref, vn_ref, slope_ref, lq_ref, sg_ref, *rest,
                        n_pages, n_steps, past_len, lam_init):
    del pt_ref
    kp_refs = rest[:n_pages]
    vp_refs = rest[n_pages:2 * n_pages]
    o_ref, qall_ref, sb_ref, m_ref, l_ref, acc_ref = rest[2 * n_pages:]
    s_idx = pl.program_id(1)
    n_rows = 2 * A_HEADS * SUBLANES
    half = A_HEADS * SUBLANES
    span = n_pages * PAGE_SIZE
    slope = slope_ref[...] * LOG2E

    def head_cols(ref, h):
        return ref[:, h * A_DV:(h + 1) * A_DV]

    @pl.when(s_idx == 0)
    def _():
        lane = lax.broadcasted_iota(jnp.int32, (SUBLANES, A_DV), 1)
        pieces = []
        for mp in range(2):
            keep = (lane >= A_HD) if mp else (lane < A_HD)
            for h in range(A_HEADS):
                pieces.append(
                    jnp.where(keep, head_cols(q_ref, h) * (A_HD ** -0.5 * LOG2E), 0.0))
        qall_ref[...] = jnp.concatenate(pieces, axis=0).astype(BF16)
        r = lax.broadcasted_iota(jnp.int32, sb_ref.shape, 0)
        c = lax.broadcasted_iota(jnp.int32, sb_ref.shape, 1)
        bias = slope * ((c >> 3) - (r & 7)).astype(F32)
        sb_ref[...] = jnp.where((c & 7) == ((r >> 3) & 7), bias, NEG_BIG)
        m_ref[...] = jnp.full(m_ref.shape, NEG_BIG, F32)
        l_ref[...] = jnp.zeros(l_ref.shape, F32)
        acc_ref[...] = jnp.zeros(acc_ref.shape, F32)

    qall = qall_ref[...]
    kcat = jnp.concatenate([r[...].astype(BF16) for r in kp_refs], axis=0)
    vcat = jnp.concatenate([r[...].astype(BF16) for r in vp_refs], axis=0)
    s = lax.dot_general(qall, kcat, _NT_DIMS, preferred_element_type=F32) + sb_ref[...]
    off = slope * (s_idx * span - past_len).astype(F32)
    _online_softmax_update(s, off, vcat, m_ref, l_ref, acc_ref)

    @pl.when(s_idx == n_steps - 1)
    def _():
        pad = jnp.zeros((PAGE_SIZE - half, A_DV), F32)
        kn = jnp.concatenate([head_cols(kn_ref, h) for h in range(A_HEADS)] + [pad], axis=0)
        vn = jnp.concatenate([head_cols(vn_ref, h) for h in range(A_HEADS)] + [pad], axis=0)
        sn = lax.dot_general(qall, kn.astype(BF16), _NT_DIMS, preferred_element_type=F32)
        r = lax.broadcasted_iota(jnp.int32, sn.shape, 0)
        c = lax.broadcasted_iota(jnp.int32, sn.shape, 1)
        qpos = r & 7
        kpos = c & 7
        valid = (c < half) & ((c >> 3) == ((r >> 3) & 7)) & (kpos <= qpos)
        sn = jnp.where(valid, sn + slope * (kpos - qpos).astype(F32), NEG_BIG)
        _online_softmax_update(sn, 0.0, vn.astype(BF16), m_ref, l_ref, acc_ref)

        o_map = acc_ref[...] / l_ref[...]
        lam = _lambda_value(lq_ref[...], lam_init)
        outs = []
        for h in range(A_HEADS):
            o = (o_map[h * SUBLANES:(h + 1) * SUBLANES]
                 - lam * o_map[half + h * SUBLANES:half + (h + 1) * SUBLANES])
            outs.append(_rms(o, sg_ref[...]) * (1.0 - lam_init))
        o_ref[...] = jnp.concatenate(outs, axis=1).astype(BF16)


def attention_sample(p3, cache_k4, cache_v4, page_table, slope_rows, lam_qk, subln_g3,
                     layer, lam_init, *, n_pages):
    b, t, _ = p3.shape
    assert t == SUBLANES
    pages_total = page_table.shape[1]
    n_steps = pages_total // n_pages
    n_rows = 2 * A_HEADS * SUBLANES
    page_rows = PAGE_SIZE * A_HEADS
    kern = functools.partial(
        _attn_sample_kernel, n_pages=n_pages, n_steps=n_steps,
        past_len=pages_total * PAGE_SIZE, lam_init=lam_init)

    def page_spec(i):
        return pl.BlockSpec(
            (None, None, page_rows, A_DV),
            lambda bi, si, pt: (layer, pt[bi, si * n_pages + i], 0, 0))

    col = lambda c: pl.BlockSpec((None, t, BRANCH_W), lambda bi, si, pt: (bi, 0, c // BRANCH_W))
    grid_spec = pltpu.PrefetchScalarGridSpec(
        num_scalar_prefetch=1,
        grid=(b, n_steps),
        in_specs=[
            col(COL_Q), col(COL_K), col(COL_V),
            pl.BlockSpec((n_rows, 1), lambda bi, si, pt: (0, 0)),
            pl.BlockSpec((None, 4, A_HD), lambda bi, si, pt: (layer, 0, 0)),
            pl.BlockSpec((None, 1, A_DV), lambda bi, si, pt: (layer, 0, 0)),
        ] + [page_spec(i) for i in range(n_pages)] * 2,
        out_specs=pl.BlockSpec((None, t, BRANCH_W), lambda bi, si, pt: (bi, 0, 0)),
        scratch_shapes=[
            pltpu.VMEM((n_rows, A_DV), BF16),
            pltpu.VMEM((n_rows, n_pages * page_rows), F32),
            pltpu.VMEM((n_rows, 1), F32),
            pltpu.VMEM((n_rows, 1), F32),
            pltpu.VMEM((n_rows, A_DV), F32),
        ],
    )
    return pl.pallas_call(
        kern,
        grid_spec=grid_spec,
        out_shape=jax.ShapeDtypeStruct((b, t, BRANCH_W), BF16),
        compiler_params=_params(("parallel", "arbitrary")),
        name="attn_sample",
    )(page_table, p3, p3, p3, slope_rows, lam_qk, subln_g3,
      *([cache_k4] * n_pages), *([cache_v4] * n_pages))


def _shift_rows(x, prev, j, row8):
    xr = pltpu.roll(x, j, 0)
    head = jnp.where(row8 < j, pltpu.roll(prev, j, 0), xr[:SUBLANES])
    if x.shape[0] == SUBLANES:
        return head
    return jnp.concatenate([head, xr[SUBLANES:]], axis=0)


def _lru_kernel(x_ref, c0_ref, h0_ref, cw_ref, cb_ref, wr_ref, wi_ref, br_ref, bi_ref, lam_ref,
                hs_ref, hl_ref, co_ref, hprev_ref, xprev_ref):
    t_idx = pl.program_id(2)

    @pl.when(t_idx == 0)
    def _():
        hprev_ref[...] = h0_ref[...]
        xprev_ref[...] = c0_ref[...]

    x = x_ref[...]
    n_t, tc = x.shape
    prev = xprev_ref[...]
    row8 = lax.broadcasted_iota(jnp.int32, (SUBLANES, tc), 0)
    cw = cw_ref[...]
    xc = cb_ref[...] + cw[0:1] * _shift_rows(x, prev, 3, row8)
    xc = xc + cw[1:2] * _shift_rows(x, prev, 2, row8)
    xc = xc + cw[2:3] * _shift_rows(x, prev, 1, row8)
    xc = xc + cw[3:4] * x
    tail = x[n_t - SUBLANES:]
    xprev_ref[...] = tail
    co_ref[...] = tail

    xcb = xc.astype(BF16)
    r = jax.nn.sigmoid(jnp.dot(xcb, wr_ref[...], preferred_element_type=F32) + br_ref[...])
    i = jax.nn.sigmoid(jnp.dot(xcb, wi_ref[...], preferred_element_type=F32) + bi_ref[...])
    nl = -lam_ref[...]
    softplus = jnp.maximum(nl, 0.0) + jnp.log1p(jnp.exp(-jnp.abs(nl)))
    log_a = (-LRU_C) * r * softplus
    a = jnp.exp(log_a)
    th = jnp.tanh(log_a)
    u = jnp.sqrt(-2.0 * th / (1.0 - th)) * (i * xc)

    n_g = n_t // SUBLANES
    a = a.reshape(n_g, SUBLANES, tc)
    u = u.reshape(n_g, SUBLANES, tc)
    row = lax.broadcasted_iota(jnp.int32, a.shape, 1)
    d = 1
    while d < SUBLANES:
        a_sh = jnp.where(row >= d, pltpu.roll(a, d, 1), 1.0)
        u_sh = jnp.where(row >= d, pltpu.roll(u, d, 1), 0.0)
        u = u + a * u_sh
        a = a * a_sh
        d *= 2
    carry = hprev_ref[...]
    groups = []
    for g in range(n_g):
        groups.append(u[g] + a[g] * carry)
        carry = groups[-1][SUBLANES - 1:]
    hs_ref[...] = jnp.concatenate(groups, axis=0).astype(BF16)
    hprev_ref[...] = carry
    hl_ref[...] = carry


def rg_lru(p3, conv0, h0, conv_w, conv_b3, wr_bd, wi_bd, b_rg3, b_ig3, lam3, layer, *, tt, tc):
    b, s, _ = p3.shape
    c = LRU_W
    vec = lambda: pl.BlockSpec((None, 1, tc), lambda bi, ci, ti: (layer, 0, ci))
    gate_w = lambda: pl.BlockSpec((None, None, tc, tc), lambda bi, ci, ti: (layer, ci, 0, 0))
    state = lambda rows: pl.BlockSpec((None, rows, tc), lambda bi, ci, ti: (bi, 0, ci))
    return pl.pallas_call(
        _lru_kernel,
        grid=(b, c // tc, s // tt),
        in_specs=[
            pl.BlockSpec((None, tt, tc), lambda bi, ci, ti: (bi, ti, COL_XB // tc + ci)),
            state(SUBLANES), state(1),
            pl.BlockSpec((None, CONV_W, tc), lambda bi, ci, ti: (layer, 0, ci)),
            vec(), gate_w(), gate_w(), vec(), vec(), vec(),
        ],
        out_specs=[
            pl.BlockSpec((None, tt, tc), lambda bi, ci, ti: (bi, ti, ci)),
            state(1), state(SUBLANES),
        ],
        out_shape=[
            jax.ShapeDtypeStruct((b, s, c), BF16),
            jax.ShapeDtypeStruct((b, 1, c), F32),
            jax.ShapeDtypeStruct((b, SUBLANES, c), F32),
        ],
        scratch_shapes=[pltpu.VMEM((1, tc), F32), pltpu.VMEM((SUBLANES, tc), F32)],
        compiler_params=_params(("parallel", "parallel", "arbitrary")),
        name="rg_lru",
    )(p3, conv0, h0, conv_w, conv_b3, wr_bd, wi_bd, b_rg3, b_ig3, lam3)


def _gelu_tanh(x):
    c = math.sqrt(2.0 / math.pi)
    return 0.5 * x * (1.0 + jnp.tanh(c * (x + 0.044715 * (x * x * x))))


def _cmlp_kernel(zu_ref, zv_ref, lg_ref, lb_ref, ws_ref, bst_ref, oc_ref, *maybe_vc_ref, ch):
    u = _gelu_tanh(zu_ref[...])
    vg = _gelu_tanh(zv_ref[...])
    xc = vg - jnp.mean(vg, axis=-1, keepdims=True)
    v = xc * lax.rsqrt(jnp.mean(xc * xc, axis=-1, keepdims=True) + EPS) * lg_ref[...] + lb_ref[...]
    if maybe_vc_ref:
        maybe_vc_ref[0][...] = v
    n_ch = u.shape[0] // ch
    r = lax.broadcasted_iota(jnp.int32, (ch, ch), 0)
    c = lax.broadcasted_iota(jnp.int32, (ch, ch), 1)
    for g in range(C_GROUPS):
        wm = jnp.where(c <= r, ws_ref[g][:ch, :ch], 0.0)
        bias = bst_ref[:ch, g:g + 1]
        cs = slice(g * C_GW, (g + 1) * C_GW)
        for n in range(n_ch):
            rs = slice(n * ch, (n + 1) * ch)
            vb = v[rs, cs]
            if ch >= LANES:
                f = jnp.dot(wm.astype(BF16), vb.astype(BF16), preferred_element_type=F32) + bias
            else:
                f = bias + wm[:, 0:1] * vb[0:1, :]
                for s in range(1, ch):
                    f = f + wm[:, s:s + 1] * vb[s:s + 1, :]
            oc_ref[rs, cs] = (u[rs, cs] * f).astype(BF16)


def chunk_mlp(p, ln_g3, ln_b3, w_s, b_st, layer, *, tm, ch, emit_v):
    m = p.shape[0]
    vec = lambda: pl.BlockSpec((None, 1, C_W), lambda i: (layer, 0, 0))
    out_specs = [pl.BlockSpec((tm, C_W), lambda i: (i, 0))]
    out_shape = [jax.ShapeDtypeStruct((m, C_W), BF16)]
    if emit_v:
        out_specs.append(pl.BlockSpec((tm, C_W), lambda i: (i, 0)))
        out_shape.append(jax.ShapeDtypeStruct((m, C_W), F32))
    return pl.pallas_call(
        functools.partial(_cmlp_kernel, ch=ch),
        grid=(m // tm,),
        in_specs=[
            pl.BlockSpec((tm, C_W), lambda i: (i, COL_Z // C_W)),
            pl.BlockSpec((tm, C_W), lambda i: (i, COL_Z // C_W + 1)),
            vec(), vec(),
            pl.BlockSpec((None, C_GROUPS, CHUNK, CHUNK), lambda i: (layer, 0, 0, 0)),
            pl.BlockSpec((None, CHUNK, C_GROUPS), lambda i: (layer, 0, 0)),
        ],
        out_specs=out_specs,
        out_shape=out_shape,
        compiler_params=_params(("parallel",)),
        name="chunk_mlp",
    )(p, p, ln_g3, ln_b3, w_s, b_st)


def _branch_kernel(oa_ref, ob_ref, oc_ref, g0_ref, g1_ref, g2_ref, wb_ref, m_ref):
    acc = None
    for j, (o_ref, g_ref) in enumerate(((oa_ref, g0_ref), (ob_ref, g1_ref), (oc_ref, g2_ref))):
        y = jnp.dot(o_ref[...], wb_ref[j], preferred_element_type=F32)
        t = g_ref[...].astype(F32) * y
        acc = t if acc is None else acc + t
    m_ref[...] = acc.astype(BF16)


def branch_merge(o_a, o_b, o_c, gates, wb, *, tm, tn):
    m = o_a.shape[0]
    o_spec = lambda: pl.BlockSpec((tm, BRANCH_W), lambda i, j: (i, 0))
    g_spec = lambda b: pl.BlockSpec((tm, tn), lambda i, j: (i, b * D_MODEL // tn + j))
    return pl.pallas_call(
        _branch_kernel,
        grid=(m // tm, D_MODEL // tn),
        in_specs=[o_spec(), o_spec(), o_spec(), g_spec(0), g_spec(1), g_spec(2),
                  pl.BlockSpec((None, N_BRANCH, BRANCH_W, tn), lambda i, j: (0, 0, 0, j))],
        out_specs=pl.BlockSpec((tm, tn), lambda i, j: (i, j)),
        out_shape=jax.ShapeDtypeStruct((m, D_MODEL), BF16),
        compiler_params=_params(("parallel", "arbitrary")),
        name="branch_merge",
    )(o_a, o_b, o_c, gates, gates, gates, wb)


def _mm_res_kernel(x_ref, w_ref, h_ref, o_ref):
    o_ref[...] = h_ref[...] + jnp.dot(x_ref[...], w_ref[...], preferred_element_type=F32)


def matmul_residual(x, w3, h, *, tm, tn):
    m, k = x.shape
    n = w3.shape[-1]
    return pl.pallas_call(
        _mm_res_kernel,
        grid=(m // tm, n // tn),
        in_specs=[
            pl.BlockSpec((tm, k), lambda i, j: (i, 0)),
            pl.BlockSpec((None, k, tn), lambda i, j: (0, 0, j)),
            pl.BlockSpec((tm, tn), lambda i, j: (i, j)),
        ],
        out_specs=pl.BlockSpec((tm, tn), lambda i, j: (i, j)),
        out_shape=jax.ShapeDtypeStruct((m, n), F32),
        compiler_params=_params(("parallel", "arbitrary")),
        name="out_proj",
    )(x, w3, h)


def _ffn_kernel(h_ref, g_ref, w1_ref, w2_ref, o_ref, xn_ref):
    @pl.when(pl.program_id(1) == 0)
    def _():
        h = h_ref[...]
        xn_ref[...] = _rms(h, g_ref[...]).astype(BF16)
        o_ref[...] = h

    hid = jnp.maximum(jnp.dot(xn_ref[...], w1_ref[...], preferred_element_type=F32), 0.0)
    o_ref[...] += jnp.dot((hid * hid).astype(BF16), w2_ref[...], preferred_element_type=F32)


def ffn(h, g3, w1, w2, layer, *, tm, tf):
    m, d = h.shape
    return pl.pallas_call(
        _ffn_kernel,
        grid=(m // tm, FF_W // tf),
        in_specs=[
            pl.BlockSpec((tm, d), lambda i, j: (i, 0)),
            pl.BlockSpec((None, 1, d), lambda i, j: (layer, 0, 0)),
            pl.BlockSpec((None, d, tf), lambda i, j: (0, 0, j)),
            pl.BlockSpec((None, tf, d), lambda i, j: (0, j, 0)),
        ],
        out_specs=pl.BlockSpec((tm, d), lambda i, j: (i, 0)),
        out_shape=jax.ShapeDtypeStruct((m, d), F32),
        scratch_shapes=[pltpu.VMEM((tm, d), BF16)],
        compiler_params=_params(("parallel", "arbitrary")),
        name="ffn",
    )(h, g3, w1, w2)


def _cast_kernel(src_ref, dst_ref):
    dst_ref[...] = src_ref[...].astype(BF16)


def cast_layer(w3, layer, *, rows):
    _, r, c = w3.shape
    return pl.pallas_call(
        _cast_kernel,
        grid=(r // rows,),
        in_specs=[pl.BlockSpec((None, rows, c), lambda i: (layer, i, 0))],
        out_specs=pl.BlockSpec((None, rows, c), lambda i: (0, i, 0)),
        out_shape=jax.ShapeDtypeStruct((1, r, c), BF16),
        compiler_params=_params(("parallel",)),
        name="cast_layer",
    )(w3)


def _final_norm_kernel(x_ref, g_ref, o_ref):
    o_ref[...] = _rms(x_ref[...], g_ref[...])


def final_norm(x, g2, *, tm):
    m, d = x.shape
    return pl.pallas_call(
        _final_norm_kernel,
        grid=(m // tm,),
        in_specs=[pl.BlockSpec((tm, d), lambda i: (i, 0)), pl.BlockSpec((1, d), lambda i: (0, 0))],
        out_specs=pl.BlockSpec((tm, d), lambda i: (i, 0)),
        out_shape=jax.ShapeDtypeStruct((m, d), F32),
        compiler_params=_params(("parallel",)),
        name="final_norm",
    )(x, g2)


def _block_diag_gates(w):
    per = 256 // LRU_BW
    w5 = w.reshape(DEPTH, LRU_BLOCKS // per, per, LRU_BW, LRU_BW)
    eye = jnp.eye(per, dtype=w.dtype)
    bd = jnp.einsum("lgbij,bc->lgbicj", w5, eye)
    return bd.reshape(DEPTH, LRU_BLOCKS // per, 256, 256).astype(BF16)


def _trunk_layer(h, layer, wts, w_in_l, lw, attend, conv0, h0, *, tiles, ch, emit_v,
                 kv_stack=None):
    bsz, seq, _ = h.shape
    m = bsz * seq
    h2 = h.reshape(m, D_MODEL)
    if kv_stack is None:
        p, gates = norm_matmul(
            h2, wts["norm1_g"], w_in_l, layer, tm=tiles["tm"], tn=tiles["tn_in"])
    else:
        p, gates, *kv_stack = norm_matmul(
            h2, wts["norm1_g"], w_in_l, layer, tm=tiles["tm"], tn=tiles["tn_in"],
            caches=tuple(kv_stack) or None, emit_cache=True)
    p3 = p.reshape(bsz, seq, COL_G)
    o_a, copies = attend(p3)
    o_a = o_a.reshape(m, BRANCH_W)
    next_w_in = None
    if lw is None:
        lw = dict(zip(("w_branch", "w_out", "w_ff1", "w_ff2"), copies))
        lw["w_branch"] = lw["w_branch"].reshape(1, N_BRANCH, BRANCH_W, D_MODEL)
        next_w_in = copies[4] if len(copies) > 4 else None
    o_b, h_last, conv_tail = rg_lru(
        p3, conv0, h0, wts["conv_w"], wts["conv_b"], wts["w_rg"], wts["w_ig"],
        wts["b_rg"], wts["b_ig"], wts["lru_lambda"], layer, tt=tiles["tt"], tc=256)
    cm = chunk_mlp(p, wts["cmlp_ln_g"], wts["cmlp_ln_b"], wts["w_s"], wts["b_st"], layer,
                   tm=tiles["tm_c"], ch=ch, emit_v=emit_v)
    mrg = branch_merge(o_a, o_b.reshape(m, BRANCH_W), cm[0], gates, lw["w_branch"],
                       tm=tiles["tm"], tn=tiles["tn_b"])
    h2 = matmul_residual(mrg, lw["w_out"], h2, tm=tiles["tm"], tn=tiles["tn"])
    h2 = ffn(h2, wts["norm2_g"], lw["w_ff1"], lw["w_ff2"], layer,
             tm=tiles["tm_f"], tf=tiles["tf"])
    if kv_stack is None:
        k = p3[..., COL_K:COL_V].reshape(bsz, seq, A_HEADS, 2 * A_HD)
        v = p3[..., COL_V:COL_XB].reshape(bsz, seq, A_HEADS, A_DV)
    else:
        k, v = kv_stack
    new_buf = conv_tail[:, SUBLANES - (CONV_W - 1):]
    v_c = cm[1].reshape(bsz, seq, C_W) if emit_v else None
    return h2.reshape(bsz, seq, D_MODEL), k, v, new_buf, h_last[:, 0], v_c, lw, next_w_in


def kernel(x_prompt, x_sample, cache_k, cache_v, state_h, state_conv, page_table,
           norm1_g, w_in, lam_qk, subln_g, conv_w, conv_b, w_rg, b_rg, w_ig, b_ig, lru_lambda,
           cmlp_ln_g, cmlp_ln_b, w_s, b_s, w_branch, w_out, norm2_g, w_ff1, w_ff2, final_g):
    bp, sp, _ = x_prompt.shape
    bs, ss, _ = x_sample.shape
    n_pool = cache_k.shape[1]
    row = lambda a: a.reshape(DEPTH, 1, a.shape[-1])
    wts = {
        "norm1_g": row(norm1_g),
        "conv_w": conv_w, "conv_b": row(conv_b),
        "w_rg": _block_diag_gates(w_rg), "w_ig": _block_diag_gates(w_ig),
        "b_rg": row(b_rg), "b_ig": row(b_ig), "lru_lambda": row(lru_lambda),
        "cmlp_ln_g": row(cmlp_ln_g), "cmlp_ln_b": row(cmlp_ln_b),
        "w_s": w_s, "b_st": jnp.swapaxes(b_s, 1, 2),
        "norm2_g": row(norm2_g),
    }
    late_views = (w_branch.reshape(DEPTH, N_BRANCH * BRANCH_W, D_MODEL), w_out, w_ff1, w_ff2)
    w_in_l = cast_layer(w_in, 0, rows=128)
    subln_g3 = row(subln_g)
    slopes = jnp.exp2(-8.0 * jnp.arange(1, A_HEADS + 1, dtype=F32) / A_HEADS)
    slope_rows = jnp.tile(jnp.repeat(slopes, SUBLANES), 2).reshape(2 * A_HEADS * SUBLANES, 1)
    cache_k4 = cache_k.reshape(DEPTH, n_pool, PAGE_SIZE * A_HEADS, A_DV)
    cache_v4 = cache_v.reshape(DEPTH, n_pool, PAGE_SIZE * A_HEADS, A_DV)
    conv0_p = jnp.zeros((bp, SUBLANES, LRU_W), F32)
    h0_p = jnp.zeros((bp, 1, LRU_W), F32)
    conv0_s = jnp.pad(state_conv, ((0, 0), (0, 0), (SUBLANES - (CONV_W - 1), 0), (0, 0)))

    tiles_p = dict(tm=1024, tn_in=1024, tn_b=512, tn=1024, tt=256, tm_c=512, tm_f=512, tf=1024)
    ms = bs * ss
    tiles_s = dict(tm=ms, tn_in=512, tn_b=512, tn=512, tt=ss, tm_c=ms, tm_f=ms, tf=1024)

    hp, hs = x_prompt, x_sample
    outs = [[] for _ in range(7)]
    kv_stack = tuple(jnp.zeros((DEPTH, bp * sp * A_HEADS, A_DV), F32) for _ in range(2))
    for l in range(DEPTH):
        lam_init = 0.8 - 0.6 * math.exp(-0.3 * l)
        cast = tuple((v, l) for v in late_views) + (((w_in, l + 1),) if l + 1 < DEPTH else ())
        attend_p = functools.partial(
            attention_prompt, slopes=slopes, lam_qk=lam_qk, subln_g3=subln_g3,
            layer=l, lam_init=lam_init, tq=256, cast=cast)
        hp, kp_stack, vp_stack, cp, hlp, _, lw, next_w_in = _trunk_layer(
            hp, l, wts, w_in_l, None, attend_p, conv0_p, h0_p, tiles=tiles_p, ch=CHUNK,
            emit_v=False, kv_stack=kv_stack)
        kv_stack = (kp_stack, vp_stack)

        def attend_s(p3, l=l, lam_init=lam_init):
            return attention_sample(
                p3, cache_k4, cache_v4, page_table, slope_rows, lam_qk, subln_g3,
                l, lam_init, n_pages=8), ()

        hs, k_s, v_s, c_s, h_s, vc_s, _, _ = _trunk_layer(
            hs, l, wts, w_in_l, lw, attend_s, conv0_s[l], state_h[l][:, None, :],
            tiles=tiles_s, ch=min(ss, CHUNK), emit_v=True)
        w_in_l = next_w_in
        for lst, val in zip(outs, (hlp, cp, k_s, v_s, h_s, c_s, vc_s)):
            lst.append(val)
    y_prompt = final_norm(hp.reshape(bp * sp, D_MODEL), final_g.reshape(1, D_MODEL), tm=1024)
    y_sample = final_norm(hs.reshape(ms, D_MODEL), final_g.reshape(1, D_MODEL), tm=ms)
    stacked = [jnp.stack(lst) for lst in outs]
    k_prompt, v_prompt = (a.reshape(DEPTH, bp, sp, A_HEADS, A_DV) for a in kv_stack)
    return (y_prompt.reshape(bp, sp, D_MODEL), y_sample.reshape(bs, ss, D_MODEL),
            k_prompt, v_prompt, *stacked)
```

```python
import functools
import math

import jax
import jax.numpy as jnp
from jax import lax
from jax.experimental import pallas as pl
from jax.experimental.pallas import tpu as pltpu

F32 = jnp.float32
BF16 = jnp.bfloat16

D_MODEL = 2048
DEPTH = 4
PAGE_SIZE = 128
BRANCH_W = D_MODEL // 2
A_HD = 64
A_HEADS = BRANCH_W // (2 * A_HD)
A_DV = 2 * A_HD
LRU_W = BRANCH_W
LRU_BLOCKS = 16
LRU_BW = LRU_W // LRU_BLOCKS
CONV_W = 4
LRU_C = 8.0
C_W = BRANCH_W
C_GROUPS = 4
C_GW = C_W // C_GROUPS
CHUNK = 128
N_BRANCH = 3
FF_W = 4 * D_MODEL
EPS = 1e-6

COL_Q = 0
COL_K = BRANCH_W
COL_V = 2 * BRANCH_W
COL_XB = 3 * BRANCH_W
COL_Z = 4 * BRANCH_W
COL_G = 6 * BRANCH_W
W_IN = COL_G + N_BRANCH * D_MODEL

LANES = 128
SUBLANES = 8
NEG_BIG = -1e30
LOG2E = math.log2(math.e)
MIB = 1024 * 1024
VMEM_LIMIT = 52 * MIB


def _params(semantics):
    return pltpu.CompilerParams(dimension_semantics=semantics, vmem_limit_bytes=VMEM_LIMIT)


def _rms(x, g):
    return x * lax.rsqrt(jnp.mean(x * x, axis=-1, keepdims=True) + EPS) * g


def _norm_mm_kernel(x_ref, g_ref, w_ref, *rest, tn, n_passthrough):
    rest = rest[n_passthrough:]
    o_ref, gate_ref, xn_ref = rest[0], rest[1], rest[-1]
    cache_refs = rest[2:-1]
    j = pl.program_id(1)
    n_main = COL_G // tn

    @pl.when(j == 0)
    def _():
        xn_ref[...] = _rms(x_ref[...], g_ref[...]).astype(BF16)

    @pl.when(j < n_main)
    def _():
        o_ref[...] = jnp.dot(xn_ref[...], w_ref[...], preferred_element_type=F32)

    @pl.when(j >= n_main)
    def _():
        pre = jnp.dot(xn_ref[...], w_ref[...], preferred_element_type=F32)
        gate_ref[...] = (0.5 * (jnp.tanh(0.5 * pre) + 1.0)).astype(BF16)

    tm = o_ref.shape[0]
    heads_per_tile = tn // A_DV
    for dst, col0 in zip(cache_refs, (COL_K, COL_V)):
        for t in range(BRANCH_W // tn):
            @pl.when(j == col0 // tn + t)
            def _(dst=dst, t=t):
                for hh in range(heads_per_tile):
                    h = t * heads_per_tile + hh
                    dst[pl.ds(h, tm, stride=A_HEADS), :] = o_ref[:, hh * A_DV:(hh + 1) * A_DV]


def norm_matmul(x, g3, w3, layer, *, tm, tn, caches=None, emit_cache=False):
    m, k = x.shape
    n = w3.shape[-1]
    n_main = COL_G // tn
    in_specs = [
        pl.BlockSpec((tm, k), lambda i, j: (i, 0), pipeline_mode=pl.Buffered(1)),
        pl.BlockSpec((None, 1, k), lambda i, j: (layer, 0, 0)),
        pl.BlockSpec((None, k, tn), lambda i, j: (0, 0, j)),
    ]
    out_specs = [
        pl.BlockSpec((tm, tn), lambda i, j: (i, jnp.minimum(j, n_main - 1))),
        pl.BlockSpec((tm, tn), lambda i, j: (i, jnp.maximum(j - n_main, 0))),
    ]
    out_shape = [jax.ShapeDtypeStruct((m, COL_G), F32),
                 jax.ShapeDtypeStruct((m, n - COL_G), BF16)]
    args = [x, g3, w3]
    aliases = {}
    if emit_cache:
        for c in range(2):
            out_specs.append(pl.BlockSpec((None, tm * A_HEADS, A_DV), lambda i, j: (layer, i, 0),
                                          pipeline_mode=pl.Buffered(1)))
            out_shape.append(jax.ShapeDtypeStruct((DEPTH, m * A_HEADS, A_DV), F32))
            if caches is not None:
                in_specs.append(pl.BlockSpec(memory_space=pl.ANY))
                aliases[len(args)] = 2 + c
                args.append(caches[c])
    kern = functools.partial(_norm_mm_kernel, tn=tn, n_passthrough=len(aliases))
    return pl.pallas_call(
        kern,
        grid=(m // tm, n // tn),
        in_specs=in_specs,
        out_specs=out_specs,
        out_shape=out_shape,
        input_output_aliases=aliases,
        scratch_shapes=[pltpu.VMEM((tm, k), BF16)],
        compiler_params=_params(("parallel", "arbitrary")),
        name="in_proj",
    )(*args)


def _lambda_value(lq, lam_init):
    a = jnp.sum(lq[0:1] * lq[1:2], axis=-1, keepdims=True)
    b = jnp.sum(lq[2:3] * lq[3:4], axis=-1, keepdims=True)
    return jnp.exp(a) - jnp.exp(b) + lam_init


def _online_softmax_update(s, off, vblk, m_ref, l_ref, acc_ref):
    m_old = m_ref[...]
    m_new = jnp.maximum(m_old, jnp.max(s, axis=-1, keepdims=True) + off)
    p = jnp.exp2(s - (m_new - off))
    alpha = jnp.exp2(m_old - m_new)
    l_ref[...] = alpha * l_ref[...] + jnp.sum(p, axis=-1, keepdims=True)
    acc_ref[...] = alpha * acc_ref[...] + jnp.dot(
        p.astype(BF16), vblk, preferred_element_type=F32)
    m_ref[...] = m_new


_NT_DIMS = (((1,), (1,)), ((), ()))


BIAS_SPLIT = 3
ONES_ROWS = 16
HEADS_PER_STEP = 4


def _bf16_pieces(x):
    out = []
    for _ in range(BIAS_SPLIT):
        hi = x.astype(BF16).astype(F32)
        out.append(hi)
        x = x - hi
    return out


def _lane_select(lane, columns):
    out = jnp.zeros_like(columns[0])
    for i, col in enumerate(columns):
        out = jnp.where(lane == i, col, out)
    return out


def _attn_prompt_kernel(slopes_ref, q_ref, k_ref, v_ref, lq_ref, sg_ref, *rest,
                        tq, lam_init, n_cast):
    src_refs = rest[:n_cast]
    o_ref = rest[n_cast]
    dst_refs = rest[n_cast + 1:2 * n_cast + 1]
    (kb_ref, vt_ref, qe_ref, mask_ref, m_ref, acc_ref, s_ref, pv_ref,
     al_ref) = rest[2 * n_cast + 1:]
    for src, dst in zip(src_refs, dst_refs):
        dst[...] = src[...].astype(BF16)
    hp = pl.program_id(1)
    qi = pl.program_id(2)
    n_blk = vt_ref.shape[1]
    seq = kb_ref.shape[1]
    slopes2 = [slopes_ref[hp * HEADS_PER_STEP + g] * LOG2E for g in range(HEADS_PER_STEP)]

    @pl.when(qi == 0)
    def _():
        lane = lax.broadcasted_iota(jnp.int32, (seq, A_DV), 1)
        kidx = (lax.broadcasted_iota(jnp.int32, (seq, A_DV), 0) & (tq - 1)).astype(F32)
        kside = jnp.where(lane < BIAS_SPLIT, kidx, jnp.where(lane < 2 * BIAS_SPLIT, 1.0, 0.0))
        qlane = lax.broadcasted_iota(jnp.int32, (2 * tq, A_DV), 1)
        qidx = (lax.broadcasted_iota(jnp.int32, (2 * tq, A_DV), 0) & (tq - 1)).astype(F32)
        for g in range(HEADS_PER_STEP):
            cs = slice(g * A_DV, (g + 1) * A_DV)
            kb_ref[g, :, :A_DV] = k_ref[:, cs].astype(BF16)
            kb_ref[g, :, A_DV:] = kside.astype(BF16)
            for jb in range(n_blk):
                vt_ref[g, jb, :A_DV, :] = v_ref[jb * tq:(jb + 1) * tq, cs].T.astype(BF16)
                vt_ref[g, jb, A_DV:, :] = jnp.ones((ONES_ROWS, tq), BF16)
            sl = jnp.full((2 * tq, A_DV), slopes2[g], F32)
            cols = _bf16_pieces(sl) + _bf16_pieces(-(sl * qidx))
            qe_ref[g] = _lane_select(qlane, cols).astype(BF16)
        r = lax.broadcasted_iota(jnp.int32, (tq, 2 * tq), 0)
        c = lax.broadcasted_iota(jnp.int32, (tq, 2 * tq), 1) & (tq - 1)
        mask_ref[...] = jnp.where(r <= c, 0.0, NEG_BIG)

    lane = lax.broadcasted_iota(jnp.int32, (tq, A_DV), 1)
    qs = []
    for g in range(HEADS_PER_STEP):
        q = q_ref[:, g * A_DV:(g + 1) * A_DV] * (A_HD ** -0.5 * LOG2E)
        q2 = jnp.concatenate(
            [jnp.where(lane < A_HD, q, 0.0), jnp.where(lane >= A_HD, q, 0.0)], axis=0)
        qs.append(jnp.concatenate([q2.astype(BF16), qe_ref[g]], axis=1))
    m_ref[...] = jnp.full(m_ref.shape, NEG_BIG, F32)
    acc_ref[...] = jnp.zeros(acc_ref.shape, F32)
    pv_ref[...] = jnp.zeros(pv_ref.shape, F32)
    al_ref[...] = jnp.ones(al_ref.shape, F32)

    def scores(g, j):
        start = pl.multiple_of(j * tq, tq)
        kblk = kb_ref[g, pl.ds(start, tq), :]
        return lax.dot_general(kblk, qs[g], _NT_DIMS, preferred_element_type=F32)

    def softmax_pv(g, j, s, off):
        m_old = m_ref[g]
        m_new = jnp.maximum(m_old, jnp.max(s, axis=0, keepdims=True) + off)
        p = jnp.exp2(s - (m_new - off))
        m_ref[g] = m_new
        pv = jnp.dot(vt_ref[g, j], p.astype(BF16), preferred_element_type=F32)
        return jnp.exp2(m_old - m_new), pv

    for g in range(HEADS_PER_STEP):
        s_ref[0, g] = scores(g, 0)

    def body(j, carry):
        rel = ((j - qi) * tq).astype(F32)
        slot = j & 1
        for g in range(HEADS_PER_STEP):
            s_cur = s_ref[slot, g]
            s_ref[1 - slot, g] = scores(g, j + 1)
            acc_ref[g] = al_ref[g] * acc_ref[g] + pv_ref[g]
            alpha, pv = softmax_pv(g, j, s_cur, slopes2[g] * rel)
            al_ref[g] = alpha
            pv_ref[g] = pv
        return carry

    lax.fori_loop(0, qi, body, 0)
    mask = mask_ref[...]
    lam = _lambda_value(lq_ref[...], lam_init)
    for g in range(HEADS_PER_STEP):
        acc = al_ref[g] * acc_ref[g] + pv_ref[g]
        alpha, pv = softmax_pv(g, qi, s_ref[qi & 1, g] + mask, 0.0)
        acc = alpha * acc + pv
        on = acc[:A_DV] / acc[A_DV:A_DV + 1]
        ot = on[:, :tq] - lam * on[:, tq:]
        yt = ot * lax.rsqrt(jnp.mean(ot * ot, axis=0, keepdims=True) + EPS)
        o_ref[:, g * A_DV:(g + 1) * A_DV] = (
            (yt.T * sg_ref[...]) * (1.0 - lam_init)).astype(BF16)


BF16_SUBLANES = 16


def _cast_rider_specs(cast, n_steps, step_of):
    in_specs, out_specs, out_shape = [], [], []
    for arr, src_layer in cast:
        _, r, c = arr.shape
        rows = pl.cdiv(pl.cdiv(r, n_steps), BF16_SUBLANES) * BF16_SUBLANES
        n_slabs = r // rows
        assert n_slabs * rows == r and n_slabs <= n_steps
        slab = lambda *idx, n_slabs=n_slabs: jnp.minimum(step_of(*idx), n_slabs - 1)
        in_specs.append(pl.BlockSpec(
            (None, rows, c), lambda *idx, slab=slab, sl=src_layer: (sl, slab(*idx), 0)))
        out_specs.append(pl.BlockSpec((None, rows, c), lambda *idx, slab=slab: (0, slab(*idx), 0)))
        out_shape.append(jax.ShapeDtypeStruct((1, r, c), BF16))
    return in_specs, out_specs, out_shape


def attention_prompt(p3, slopes, lam_qk, subln_g3, layer, lam_init, *, tq, cast=()):
    b, s, _ = p3.shape
    gw = HEADS_PER_STEP * A_DV
    n_hg, n_q = A_HEADS // HEADS_PER_STEP, s // tq
    cast_in, cast_out, cast_shape = _cast_rider_specs(
        cast, b * n_hg * n_q, lambda bi, h, qi: (bi * n_hg + h) * n_q + qi)
    kern = functools.partial(_attn_prompt_kernel, tq=tq, lam_init=lam_init, n_cast=len(cast))
    outs = pl.pallas_call(
        kern,
        grid=(b, n_hg, n_q),
        in_specs=[
            pl.BlockSpec(memory_space=pltpu.SMEM),
            pl.BlockSpec((None, tq, gw), lambda bi, h, qi: (bi, qi, COL_Q // gw + h)),
            pl.BlockSpec((None, s, gw), lambda bi, h, qi: (bi, 0, COL_K // gw + h),
                         pipeline_mode=pl.Buffered(1)),
            pl.BlockSpec((None, s, gw), lambda bi, h, qi: (bi, 0, COL_V // gw + h),
                         pipeline_mode=pl.Buffered(1)),
            pl.BlockSpec((None, 4, A_HD), lambda bi, h, qi: (layer, 0, 0)),
            pl.BlockSpec((None, 1, A_DV), lambda bi, h, qi: (layer, 0, 0)),
        ] + cast_in,
        out_specs=[pl.BlockSpec((None, tq, gw), lambda bi, h, qi: (bi, qi, h))] + cast_out,
        out_shape=[jax.ShapeDtypeStruct((b, s, BRANCH_W), BF16)] + cast_shape,
        scratch_shapes=[
            pltpu.VMEM((HEADS_PER_STEP, s, 2 * A_DV), BF16),
            pltpu.VMEM((HEADS_PER_STEP, s // tq, A_DV + ONES_ROWS, tq), BF16),
            pltpu.VMEM((HEADS_PER_STEP, 2 * tq, A_DV), BF16),
            pltpu.VMEM((tq, 2 * tq), F32),
            pltpu.VMEM((HEADS_PER_STEP, 1, 2 * tq), F32),
            pltpu.VMEM((HEADS_PER_STEP, A_DV + ONES_ROWS, 2 * tq), F32),
            pltpu.VMEM((2, HEADS_PER_STEP, tq, 2 * tq), F32),
            pltpu.VMEM((HEADS_PER_STEP, A_DV + ONES_ROWS, 2 * tq), F32),
            pltpu.VMEM((HEADS_PER_STEP, 1, 2 * tq), F32),
        ],
        compiler_params=_params(("parallel", "arbitrary", "arbitrary")),
        name="attn_prompt",
    )(slopes, p3, p3, p3, lam_qk, subln_g3, *[arr for arr, _ in cast])
    return outs[0], outs[1:]


def _attn_sample_kernel(pt_ref, q_ref, kn_ref, vn_ref, slope_ref, lq_ref, sg_ref, *rest,
                        n_pages, n_steps, past_len, lam_init):
    del pt_ref
    kp_refs = rest[:n_pages]
    vp_refs = rest[n_pages:2 * n_pages]
    o_ref, qall_ref, sb_ref, m_ref, l_ref, acc_ref = rest[2 * n_pages:]
    s_idx = pl.program_id(1)
    n_rows = 2 * A_HEADS * SUBLANES
    half = A_HEADS * SUBLANES
    span = n_pages * PAGE_SIZE
    slope = slope_ref[...] * LOG2E

    def head_cols(ref, h):
        return ref[:, h * A_DV:(h + 1) * A_DV]

    @pl.when(s_idx == 0)
    def _():
        lane = lax.broadcasted_iota(jnp.int32, (SUBLANES, A_DV), 1)
        pieces = []
        for mp in range(2):
            keep = (lane >= A_HD) if mp else (lane < A_HD)
            for h in range(A_HEADS):
                pieces.append(
                    jnp.where(keep, head_cols(q_ref, h) * (A_HD ** -0.5 * LOG2E), 0.0))
        qall_ref[...] = jnp.concatenate(pieces, axis=0).astype(BF16)
        r = lax.broadcasted_iota(jnp.int32, sb_ref.shape, 0)
        c = lax.broadcasted_iota(jnp.int32, sb_ref.shape, 1)
        bias = slope * ((c >> 3) - (r & 7)).astype(F32)
        sb_ref[...] = jnp.where((c & 7) == ((r >> 3) & 7), bias, NEG_BIG)
        m_ref[...] = jnp.full(m_ref.shape, NEG_BIG, F32)
        l_ref[...] = jnp.zeros(l_ref.shape, F32)
        acc_ref[...] = jnp.zeros(acc_ref.shape, F32)

    qall = qall_ref[...]
    kcat = jnp.concatenate([r[...].astype(BF16) for r in kp_refs], axis=0)
    vcat = jnp.concatenate([r[...].astype(BF16) for r in vp_refs], axis=0)
    s = lax.dot_general(qall, kcat, _NT_DIMS, preferred_element_type=F32) + sb_ref[...]
    off = slope * (s_idx * span - past_len).astype(F32)
    _online_softmax_update(s, off, vcat, m_ref, l_ref, acc_ref)

    @pl.when(s_idx == n_steps - 1)
    def _():
        pad = jnp.zeros((PAGE_SIZE - half, A_DV), F32)
        kn = jnp.concatenate([head_cols(kn_ref, h) for h in range(A_HEADS)] + [pad], axis=0)
        vn = jnp.concatenate([head_cols(vn_ref, h) for h in range(A_HEADS)] + [pad], axis=0)
        sn = lax.dot_general(qall, kn.astype(BF16), _NT_DIMS, preferred_element_type=F32)
        r = lax.broadcasted_iota(jnp.int32, sn.shape, 0)
        c = lax.broadcasted_iota(jnp.int32, sn.shape, 1)
        qpos = r & 7
        kpos = c & 7
        valid = (c < half) & ((c >> 3) == ((r >> 3) & 7)) & (kpos <= qpos)
        sn = jnp.where(valid, sn + slope * (kpos - qpos).astype(F32), NEG_BIG)
        _online_softmax_update(sn, 0.0, vn.astype(BF16), m_ref, l_ref, acc_ref)

        o_map = acc_ref[...] / l_ref[...]
        lam = _lambda_value(lq_ref[...], lam_init)
        outs = []
        for h in range(A_HEADS):
            o = (o_map[h * SUBLANES:(h + 1) * SUBLANES]
                 - lam * o_map[half + h * SUBLANES:half + (h + 1) * SUBLANES])
            outs.append(_rms(o, sg_ref[...]) * (1.0 - lam_init))
        o_ref[...] = jnp.concatenate(outs, axis=1).astype(BF16)


def attention_sample(p3, cache_k4, cache_v4, page_table, slope_rows, lam_qk, subln_g3,
                     layer, lam_init, *, n_pages):
    b, t, _ = p3.shape
    assert t == SUBLANES
    pages_total = page_table.shape[1]
    n_steps = pages_total // n_pages
    n_rows = 2 * A_HEADS * SUBLANES
    page_rows = PAGE_SIZE * A_HEADS
    kern = functools.partial(
        _attn_sample_kernel, n_pages=n_pages, n_steps=n_steps,
        past_len=pages_total * PAGE_SIZE, lam_init=lam_init)

    def page_spec(i):
        return pl.BlockSpec(
            (None, None, page_rows, A_DV),
            lambda bi, si, pt: (layer, pt[bi, si * n_pages + i], 0, 0))

    col = lambda c: pl.BlockSpec((None, t, BRANCH_W), lambda bi, si, pt: (bi, 0, c // BRANCH_W))
    grid_spec = pltpu.PrefetchScalarGridSpec(
        num_scalar_prefetch=1,
        grid=(b, n_steps),
        in_specs=[
            col(COL_Q), col(COL_K), col(COL_V),
            pl.BlockSpec((n_rows, 1), lambda bi, si, pt: (0, 0)),
            pl.BlockSpec((None, 4, A_HD), lambda bi, si, pt: (layer, 0, 0)),
            pl.BlockSpec((None, 1, A_DV), lambda bi, si, pt: (layer, 0, 0)),
        ] + [page_spec(i) for i in range(n_pages)] * 2,
        out_specs=pl.BlockSpec((None, t, BRANCH_W), lambda bi, si, pt: (bi, 0, 0)),
        scratch_shapes=[
            pltpu.VMEM((n_rows, A_DV), BF16),
            pltpu.VMEM((n_rows, n_pages * page_rows), F32),
            pltpu.VMEM((n_rows, 1), F32),
            pltpu.VMEM((n_rows, 1), F32),
            pltpu.VMEM((n_rows, A_DV), F32),
        ],
    )
    return pl.pallas_call(
        kern,
        grid_spec=grid_spec,
        out_shape=jax.ShapeDtypeStruct((b, t, BRANCH_W), BF16),
        compiler_params=_params(("parallel", "arbitrary")),
        name="attn_sample",
    )(page_table, p3, p3, p3, slope_rows, lam_qk, subln_g3,
      *([cache_k4] * n_pages), *([cache_v4] * n_pages))


def _shift_rows(x, prev, j, row8):
    xr = pltpu.roll(x, j, 0)
    head = jnp.where(row8 < j, pltpu.roll(prev, j, 0), xr[:SUBLANES])
    if x.shape[0] == SUBLANES:
        return head
    return jnp.concatenate([head, xr[SUBLANES:]], axis=0)


def _lru_kernel(x_ref, c0_ref, h0_ref, cw_ref, cb_ref, wr_ref, wi_ref, br_ref, bi_ref, lam_ref,
                hs_ref, hl_ref, co_ref, hprev_ref, xprev_ref):
    t_idx = pl.program_id(2)

    @pl.when(t_idx == 0)
    def _():
        hprev_ref[...] = h0_ref[...]
        xprev_ref[...] = c0_ref[...]

    x = x_ref[...]
    n_t, tc = x.shape
    prev = xprev_ref[...]
    row8 = lax.broadcasted_iota(jnp.int32, (SUBLANES, tc), 0)
    cw = cw_ref[...]
    xc = cb_ref[...] + cw[0:1] * _shift_rows(x, prev, 3, row8)
    xc = xc + cw[1:2] * _shift_rows(x, prev, 2, row8)
    xc = xc + cw[2:3] * _shift_rows(x, prev, 1, row8)
    xc = xc + cw[3:4] * x
    tail = x[n_t - SUBLANES:]
    xprev_ref[...] = tail
    co_ref[...] = tail

    xcb = xc.astype(BF16)
    r = jax.nn.sigmoid(jnp.dot(xcb, wr_ref[...], preferred_element_type=F32) + br_ref[...])
    i = jax.nn.sigmoid(jnp.dot(xcb, wi_ref[...], preferred_element_type=F32) + bi_ref[...])
    nl = -lam_ref[...]
    softplus = jnp.maximum(nl, 0.0) + jnp.log1p(jnp.exp(-jnp.abs(nl)))
    log_a = (-LRU_C) * r * softplus
    a = jnp.exp(log_a)
    th = jnp.tanh(log_a)
    u = jnp.sqrt(-2.0 * th / (1.0 - th)) * (i * xc)

    n_g = n_t // SUBLANES
    a = a.reshape(n_g, SUBLANES, tc)
    u = u.reshape(n_g, SUBLANES, tc)
    row = lax.broadcasted_iota(jnp.int32, a.shape, 1)
    d = 1
    while d < SUBLANES:
        a_sh = jnp.where(row >= d, pltpu.roll(a, d, 1), 1.0)
        u_sh = jnp.where(row >= d, pltpu.roll(u, d, 1), 0.0)
        u = u + a * u_sh
        a = a * a_sh
        d *= 2
    carry = hprev_ref[...]
    groups = []
    for g in range(n_g):
        groups.append(u[g] + a[g] * carry)
        carry = groups[-1][SUBLANES - 1:]
    hs_ref[...] = jnp.concatenate(groups, axis=0).astype(BF16)
    hprev_ref[...] = carry
    hl_ref[...] = carry


def rg_lru(p3, conv0, h0, conv_w, conv_b3, wr_bd, wi_bd, b_rg3, b_ig3, lam3, layer, *, tt, tc):
    b, s, _ = p3.shape
    c = LRU_W
    vec = lambda: pl.BlockSpec((None, 1, tc), lambda bi, ci, ti: (layer, 0, ci))
    gate_w = lambda: pl.BlockSpec((None, None, tc, tc), lambda bi, ci, ti: (layer, ci, 0, 0))
    state = lambda rows: pl.BlockSpec((None, rows, tc), lambda bi, ci, ti: (bi, 0, ci))
    return pl.pallas_call(
        _lru_kernel,
        grid=(b, c // tc, s // tt),
        in_specs=[
            pl.BlockSpec((None, tt, tc), lambda bi, ci, ti: (bi, ti, COL_XB // tc + ci)),
            state(SUBLANES), state(1),
            pl.BlockSpec((None, CONV_W, tc), lambda bi, ci, ti: (layer, 0, ci)),
            vec(), gate_w(), gate_w(), vec(), vec(), vec(),
        ],
        out_specs=[
            pl.BlockSpec((None, tt, tc), lambda bi, ci, ti: (bi, ti, ci)),
            state(1), state(SUBLANES),
        ],
        out_shape=[
            jax.ShapeDtypeStruct((b, s, c), BF16),
            jax.ShapeDtypeStruct((b, 1, c), F32),
            jax.ShapeDtypeStruct((b, SUBLANES, c), F32),
        ],
        scratch_shapes=[pltpu.VMEM((1, tc), F32), pltpu.VMEM((SUBLANES, tc), F32)],
        compiler_params=_params(("parallel", "parallel", "arbitrary")),
        name="rg_lru",
    )(p3, conv0, h0, conv_w, conv_b3, wr_bd, wi_bd, b_rg3, b_ig3, lam3)


def _gelu_tanh(x):
    c = math.sqrt(2.0 / math.pi)
    return 0.5 * x * (1.0 + jnp.tanh(c * (x + 0.044715 * (x * x * x))))


def _cmlp_kernel(zu_ref, zv_ref, lg_ref, lb_ref, ws_ref, bst_ref, oc_ref, *maybe_vc_ref, ch):
    u = _gelu_tanh(zu_ref[...])
    vg = _gelu_tanh(zv_ref[...])
    xc = vg - jnp.mean(vg, axis=-1, keepdims=True)
    v = xc * lax.rsqrt(jnp.mean(xc * xc, axis=-1, keepdims=True) + EPS) * lg_ref[...] + lb_ref[...]
    if maybe_vc_ref:
        maybe_vc_ref[0][...] = v
    n_ch = u.shape[0] // ch
    r = lax.broadcasted_iota(jnp.int32, (ch, ch), 0)
    c = lax.broadcasted_iota(jnp.int32, (ch, ch), 1)
    for g in range(C_GROUPS):
        wm = jnp.where(c <= r, ws_ref[g][:ch, :ch], 0.0)
        bias = bst_ref[:ch, g:g + 1]
        cs = slice(g * C_GW, (g + 1) * C_GW)
        for n in range(n_ch):
            rs = slice(n * ch, (n + 1) * ch)
            vb = v[rs, cs]
            if ch >= LANES:
                f = jnp.dot(wm.astype(BF16), vb.astype(BF16), preferred_element_type=F32) + bias
            else:
                f = bias + wm[:, 0:1] * vb[0:1, :]
                for s in range(1, ch):
                    f = f + wm[:, s:s + 1] * vb[s:s + 1, :]
            oc_ref[rs, cs] = (u[rs, cs] * f).astype(BF16)


def chunk_mlp(p, ln_g3, ln_b3, w_s, b_st, layer, *, tm, ch, emit_v):
    m = p.shape[0]
    vec = lambda: pl.BlockSpec((None, 1, C_W), lambda i: (layer, 0, 0))
    out_specs = [pl.BlockSpec((tm, C_W), lambda i: (i, 0))]
    out_shape = [jax.ShapeDtypeStruct((m, C_W), BF16)]
    if emit_v:
        out_specs.append(pl.BlockSpec((tm, C_W), lambda i: (i, 0)))
        out_shape.append(jax.ShapeDtypeStruct((m, C_W), F32))
    return pl.pallas_call(
        functools.partial(_cmlp_kernel, ch=ch),
        grid=(m // tm,),
        in_specs=[
            pl.BlockSpec((tm, C_W), lambda i: (i, COL_Z // C_W)),
            pl.BlockSpec((tm, C_W), lambda i: (i, COL_Z // C_W + 1)),
            vec(), vec(),
            pl.BlockSpec((None, C_GROUPS, CHUNK, CHUNK), lambda i: (layer, 0, 0, 0)),
            pl.BlockSpec((None, CHUNK, C_GROUPS), lambda i: (layer, 0, 0)),
        ],
        out_specs=out_specs,
        out_shape=out_shape,
        compiler_params=_params(("parallel",)),
        name="chunk_mlp",
    )(p, p, ln_g3, ln_b3, w_s, b_st)


def _branch_kernel(oa_ref, ob_ref, oc_ref, g0_ref, g1_ref, g2_ref, wb_ref, m_ref):
    acc = None
    for j, (o_ref, g_ref) in enumerate(((oa_ref, g0_ref), (ob_ref, g1_ref), (oc_ref, g2_ref))):
        y = jnp.dot(o_ref[...], wb_ref[j], preferred_element_type=F32)
        t = g_ref[...].astype(F32) * y
        acc = t if acc is None else acc + t
    m_ref[...] = acc.astype(BF16)


def branch_merge(o_a, o_b, o_c, gates, wb, *, tm, tn):
    m = o_a.shape[0]
    o_spec = lambda: pl.BlockSpec((tm, BRANCH_W), lambda i, j: (i, 0))
    g_spec = lambda b: pl.BlockSpec((tm, tn), lambda i, j: (i, b * D_MODEL // tn + j))
    return pl.pallas_call(
        _branch_kernel,
        grid=(m // tm, D_MODEL // tn),
        in_specs=[o_spec(), o_spec(), o_spec(), g_spec(0), g_spec(1), g_spec(2),
                  pl.BlockSpec((None, N_BRANCH, BRANCH_W, tn), lambda i, j: (0, 0, 0, j))],
        out_specs=pl.BlockSpec((tm, tn), lambda i, j: (i, j)),
        out_shape=jax.ShapeDtypeStruct((m, D_MODEL), BF16),
        compiler_params=_params(("parallel", "arbitrary")),
        name="branch_merge",
    )(o_a, o_b, o_c, gates, gates, gates, wb)


def _mm_res_kernel(x_ref, w_ref, h_ref, o_ref):
    o_ref[...] = h_ref[...] + jnp.dot(x_ref[...], w_ref[...], preferred_element_type=F32)


def matmul_residual(x, w3, h, *, tm, tn):
    m, k = x.shape
    n = w3.shape[-1]
    return pl.pallas_call(
        _mm_res_kernel,
        grid=(m // tm, n // tn),
        in_specs=[
            pl.BlockSpec((tm, k), lambda i, j: (i, 0)),
            pl.BlockSpec((None, k, tn), lambda i, j: (0, 0, j)),
            pl.BlockSpec((tm, tn), lambda i, j: (i, j)),
        ],
        out_specs=pl.BlockSpec((tm, tn), lambda i, j: (i, j)),
        out_shape=jax.ShapeDtypeStruct((m, n), F32),
        compiler_params=_params(("parallel", "arbitrary")),
        name="out_proj",
    )(x, w3, h)


def _ffn_kernel(h_ref, g_ref, w1_ref, w2_ref, o_ref, xn_ref):
    @pl.when(pl.program_id(1) == 0)
    def _():
        h = h_ref[...]
        xn_ref[...] = _rms(h, g_ref[...]).astype(BF16)
        o_ref[...] = h

    hid = jnp.maximum(jnp.dot(xn_ref[...], w1_ref[...], preferred_element_type=F32), 0.0)
    o_ref[...] += jnp.dot((hid * hid).astype(BF16), w2_ref[...], preferred_element_type=F32)


def ffn(h, g3, w1, w2, layer, *, tm, tf):
    m, d = h.shape
    return pl.pallas_call(
        _ffn_kernel,
        grid=(m // tm, FF_W // tf),
        in_specs=[
            pl.BlockSpec((tm, d), lambda i, j: (i, 0)),
            pl.BlockSpec((None, 1, d), lambda i, j: (layer, 0, 0)),
            pl.BlockSpec((None, d, tf), lambda i, j: (0, 0, j)),
            pl.BlockSpec((None, tf, d), lambda i, j: (0, j, 0)),
        ],
        out_specs=pl.BlockSpec((tm, d), lambda i, j: (i, 0)),
        out_shape=jax.ShapeDtypeStruct((m, d), F32),
        scratch_shapes=[pltpu.VMEM((tm, d), BF16)],
        compiler_params=_params(("parallel", "arbitrary")),
        name="ffn",
    )(h, g3, w1, w2)


def _cast_kernel(src_ref, dst_ref):
    dst_ref[...] = src_ref[...].astype(BF16)


def cast_layer(w3, layer, *, rows):
    _, r, c = w3.shape
    return pl.pallas_call(
        _cast_kernel,
        grid=(r // rows,),
        in_specs=[pl.BlockSpec((None, rows, c), lambda i: (layer, i, 0))],
        out_specs=pl.BlockSpec((None, rows, c), lambda i: (0, i, 0)),
        out_shape=jax.ShapeDtypeStruct((1, r, c), BF16),
        compiler_params=_params(("parallel",)),
        name="cast_layer",
    )(w3)


def _final_norm_kernel(x_ref, g_ref, o_ref):
    o_ref[...] = _rms(x_ref[...], g_ref[...])


def final_norm(x, g2, *, tm):
    m, d = x.shape
    return pl.pallas_call(
        _final_norm_kernel,
        grid=(m // tm,),
        in_specs=[pl.BlockSpec((tm, d), lambda i: (i, 0)), pl.BlockSpec((1, d), lambda i: (0, 0))],
        out_specs=pl.BlockSpec((tm, d), lambda i: (i, 0)),
        out_shape=jax.ShapeDtypeStruct((m, d), F32),
        compiler_params=_params(("parallel",)),
        name="final_norm",
    )(x, g2)


def _block_diag_gates(w):
    per = 256 // LRU_BW
    w5 = w.reshape(DEPTH, LRU_BLOCKS // per, per, LRU_BW, LRU_BW)
    eye = jnp.eye(per, dtype=w.dtype)
    bd = jnp.einsum("lgbij,bc->lgbicj", w5, eye)
    return bd.reshape(DEPTH, LRU_BLOCKS // per, 256, 256).astype(BF16)


def _trunk_layer(h, layer, wts, w_in_l, lw, attend, conv0, h0, *, tiles, ch, emit_v,
                 kv_stack=None):
    bsz, seq, _ = h.shape
    m = bsz * seq
    h2 = h.reshape(m, D_MODEL)
    if kv_stack is None:
        p, gates = norm_matmul(
            h2, wts["norm1_g"], w_in_l, layer, tm=tiles["tm"], tn=tiles["tn_in"])
    else:
        p, gates, *kv_stack = norm_matmul(
            h2, wts["norm1_g"], w_in_l, layer, tm=tiles["tm"], tn=tiles["tn_in"],
            caches=tuple(kv_stack) or None, emit_cache=True)
    p3 = p.reshape(bsz, seq, COL_G)
    o_a, copies = attend(p3)
    o_a = o_a.reshape(m, BRANCH_W)
    next_w_in = None
    if lw is None:
        lw = dict(zip(("w_branch", "w_out", "w_ff1", "w_ff2"), copies))
        lw["w_branch"] = lw["w_branch"].reshape(1, N_BRANCH, BRANCH_W, D_MODEL)
        next_w_in = copies[4] if len(copies) > 4 else None
    o_b, h_last, conv_tail = rg_lru(
        p3, conv0, h0, wts["conv_w"], wts["conv_b"], wts["w_rg"], wts["w_ig"],
        wts["b_rg"], wts["b_ig"], wts["lru_lambda"], layer, tt=tiles["tt"], tc=256)
    cm = chunk_mlp(p, wts["cmlp_ln_g"], wts["cmlp_ln_b"], wts["w_s"], wts["b_st"], layer,
                   tm=tiles["tm_c"], ch=ch, emit_v=emit_v)
    mrg = branch_merge(o_a, o_b.reshape(m, BRANCH_W), cm[0], gates, lw["w_branch"],
                       tm=tiles["tm"], tn=tiles["tn_b"])
    h2 = matmul_residual(mrg, lw["w_out"], h2, tm=tiles["tm"], tn=tiles["tn"])
    h2 = ffn(h2, wts["norm2_g"], lw["w_ff1"], lw["w_ff2"], layer,
             tm=tiles["tm_f"], tf=tiles["tf"])
    if kv_stack is None:
        k = p3[..., COL_K:COL_V].reshape(bsz, seq, A_HEADS, 2 * A_HD)
        v = p3[..., COL_V:COL_XB].reshape(bsz, seq, A_HEADS, A_DV)
    else:
        k, v = kv_stack
    new_buf = conv_tail[:, SUBLANES - (CONV_W - 1):]
    v_c = cm[1].reshape(bsz, seq, C_W) if emit_v else None
    return h2.reshape(bsz, seq, D_MODEL), k, v, new_buf, h_last[:, 0], v_c, lw, next_w_in


def kernel(x_prompt, x_sample, cache_k, cache_v, state_h, state_conv, page_table,
           norm1_g, w_in, lam_qk, subln_g, conv_w, conv_b, w_rg, b_rg, w_ig, b_ig, lru_lambda,
           cmlp_ln_g, cmlp_ln_b, w_s, b_s, w_branch, w_out, norm2_g, w_ff1, w_ff2, final_g):
    bp, sp, _ = x_prompt.shape
    bs, ss, _ = x_sample.shape
    n_pool = cache_k.shape[1]
    row = lambda a: a.reshape(DEPTH, 1, a.shape[-1])
    wts = {
        "norm1_g": row(norm1_g),
        "conv_w": conv_w, "conv_b": row(conv_b),
        "w_rg": _block_diag_gates(w_rg), "w_ig": _block_diag_gates(w_ig),
        "b_rg": row(b_rg), "b_ig": row(b_ig), "lru_lambda": row(lru_lambda),
        "cmlp_ln_g": row(cmlp_ln_g), "cmlp_ln_b": row(cmlp_ln_b),
        "w_s": w_s, "b_st": jnp.swapaxes(b_s, 1, 2),
        "norm2_g": row(norm2_g),
    }
    late_views = (w_branch.reshape(DEPTH, N_BRANCH * BRANCH_W, D_MODEL), w_out, w_ff1, w_ff2)
    w_in_l = cast_layer(w_in, 0, rows=128)
    subln_g3 = row(subln_g)
    slopes = jnp.exp2(-8.0 * jnp.arange(1, A_HEADS + 1, dtype=F32) / A_HEADS)
    slope_rows = jnp.tile(jnp.repeat(slopes, SUBLANES), 2).reshape(2 * A_HEADS * SUBLANES, 1)
    cache_k4 = cache_k.reshape(DEPTH, n_pool, PAGE_SIZE * A_HEADS, A_DV)
    cache_v4 = cache_v.reshape(DEPTH, n_pool, PAGE_SIZE * A_HEADS, A_DV)
    conv0_p = jnp.zeros((bp, SUBLANES, LRU_W), F32)
    h0_p = jnp.zeros((bp, 1, LRU_W), F32)
    conv0_s = jnp.pad(state_conv, ((0, 0), (0, 0), (SUBLANES - (CONV_W - 1), 0), (0, 0)))

    tiles_p = dict(tm=1024, tn_in=1024, tn_b=512, tn=1024, tt=512, tm_c=512, tm_f=512, tf=1024)
    ms = bs * ss
    tiles_s = dict(tm=ms, tn_in=512, tn_b=512, tn=512, tt=ss, tm_c=ms, tm_f=ms, tf=1024)

    hp, hs = x_prompt, x_sample
    outs = [[] for _ in range(7)]
    kv_stack = tuple(jnp.zeros((DEPTH, bp * sp * A_HEADS, A_DV), F32) for _ in range(2))
    for l in range(DEPTH):
        lam_init = 0.8 - 0.6 * math.exp(-0.3 * l)
        cast = tuple((v, l) for v in late_views) + (((w_in, l + 1),) if l + 1 < DEPTH else ())
        attend_p = functools.partial(
            attention_prompt, slopes=slopes, lam_qk=lam_qk, subln_g3=subln_g3,
            layer=l, lam_init=lam_init, tq=256, cast=cast)
        hp, kp_stack, vp_stack, cp, hlp, _, lw, next_w_in = _trunk_layer(
            hp, l, wts, w_in_l, None, attend_p, conv0_p, h0_p, tiles=tiles_p, ch=CHUNK,
            emit_v=False, kv_stack=kv_stack)
        kv_stack = (kp_stack, vp_stack)

        def attend_s(p3, l=l, lam_init=lam_init):
            return attention_sample(
                p3, cache_k4, cache_v4, page_table, slope_rows, lam_qk, subln_g3,
                l, lam_init, n_pages=8), ()

        hs, k_s, v_s, c_s, h_s, vc_s, _, _ = _trunk_layer(
            hs, l, wts, w_in_l, lw, attend_s, conv0_s[l], state_h[l][:, None, :],
            tiles=tiles_s, ch=min(ss, CHUNK), emit_v=True)
        w_in_l = next_w_in
        for lst, val in zip(outs, (hlp, cp, k_s, v_s, h_s, c_s, vc_s)):
            lst.append(val)
    y_prompt = final_norm(hp.reshape(bp * sp, D_MODEL), final_g.reshape(1, D_MODEL), tm=1024)
    y_sample = final_norm(hs.reshape(ms, D_MODEL), final_g.reshape(1, D_MODEL), tm=ms)
    stacked = [jnp.stack(lst) for lst in outs]
    k_prompt, v_prompt = (a.reshape(DEPTH, bp, sp, A_HEADS, A_DV) for a in kv_stack)
    return (y_prompt.reshape(bp, sp, D_MODEL), y_sample.reshape(bs, ss, D_MODEL),
            k_prompt, v_prompt, *stacked)
```
